```python
import jax, jax.numpy as jnp
from jax import lax
import numpy as np

D_MODEL = 1024
BATCH = 8
SEQ = 4096
DEPTH = 2

RET_HEADS = 4
RET_DIM = 128
D_RET = RET_HEADS * RET_DIM
MLSTM_HEADS = 4
MLSTM_DIM = 128
D_MLSTM = MLSTM_HEADS * MLSTM_DIM
D_MIX = D_RET + D_MLSTM
D_FF = 2816
CHUNK = 128
CONV_WIDTH = 4
ROPE_BASE = 10000.0
EPS = 1e-6
N_SUBLAYERS = 3
D_IN = 4 * D_RET + 4 * D_MLSTM + 2 * MLSTM_HEADS

kernel_name = "hybrid_retention_mlstm_macaron_adaln"


def rmsnorm(x, g):
    xf = x.astype(jnp.float32)
    xf = xf * lax.rsqrt(jnp.mean(xf * xf, axis=-1, keepdims=True) + EPS)
    return xf.astype(x.dtype) * g


def ada_mod(c, w, b):
    mod = jax.nn.silu(c) @ w + b
    shift, scale, gate = jnp.split(mod, 3, axis=-1)
    return shift[:, None, :], scale[:, None, :], gate[:, None, :]


def swiglu(h, w1, w3, w2):
    return (jax.nn.silu(h @ w1) * (h @ w3)) @ w2


def to_heads(t, n_heads):
    b, s, _ = t.shape
    return t.reshape(b, s, n_heads, -1).transpose(0, 2, 1, 3).astype(jnp.float32)


def head_norm(h, g):
    mu = jnp.mean(h, axis=-1, keepdims=True)
    var = jnp.mean(jnp.square(h - mu), axis=-1, keepdims=True)
    hn = (h - mu) * lax.rsqrt(var + EPS)
    b, nh, s, dh = hn.shape
    return hn.transpose(0, 2, 1, 3).reshape(b, s, nh * dh) * g.astype(jnp.float32)


def rotary(t, positions):
    dh = t.shape[-1]
    inv_freq = ROPE_BASE ** (-jnp.arange(0, dh, 2, dtype=jnp.float32) / dh)
    ang = positions.astype(jnp.float32)[:, None, :, None] * inv_freq
    cos, sin = jnp.cos(ang), jnp.sin(ang)
    t1, t2 = jnp.split(t, 2, axis=-1)
    return jnp.concatenate([t1 * cos - t2 * sin, t1 * sin + t2 * cos], axis=-1)


def causal_depthwise_conv(u, w, b):
    s = u.shape[1]
    up = jnp.pad(u, ((0, 0), (CONV_WIDTH - 1, 0), (0, 0)))
    out = b
    for j in range(CONV_WIDTH):
        out = out + w[j] * up[:, j:j + s, :]
    return out


def retention_chunkwise(q, k, v):
    b, nh, s, dh = q.shape
    nc = s // CHUNK
    k = k * (dh ** -0.5)
    log_gamma = jnp.log(1.0 - 2.0 ** (-5.0 - jnp.arange(nh, dtype=jnp.float32)))
    idx = jnp.arange(CHUNK)
    diff = (idx[:, None] - idx[None, :]).astype(jnp.float32)
    decay = jnp.where(diff >= 0, jnp.exp(log_gamma[:, None, None] * jnp.maximum(diff, 0.0)), 0.0)
    qc = q.reshape(b, nh, nc, CHUNK, dh)
    kc = k.reshape(b, nh, nc, CHUNK, dh)
    vc = v.reshape(b, nh, nc, CHUNK, dh)
    scores = jnp.einsum('bhcld,bhcmd->bhclm', qc, kc) * decay[None, :, None]
    o_intra = jnp.einsum('bhclm,bhcmd->bhcld', scores, vc)
    w_k = jnp.exp(log_gamma[:, None] * (CHUNK - 1 - idx).astype(jnp.float32))
    kv = jnp.einsum('bhcld,bhcle->bhcde', kc * w_k[None, :, None, :, None], vc)
    chunk_decay = jnp.exp(log_gamma * CHUNK)[:, None, None]

    def step(state, kv_c):
        return state * chunk_decay + kv_c, state

    _, prev = lax.scan(step, jnp.zeros((b, nh, dh, dh), jnp.float32), jnp.moveaxis(kv, 2, 0))
    prev = jnp.moveaxis(prev, 0, 2)
    w_q = jnp.exp(log_gamma[:, None] * (idx + 1).astype(jnp.float32))
    o_inter = jnp.einsum('bhcld,bhcde->bhcle', qc * w_q[None, :, None, :, None], prev)
    return (o_intra + o_inter).reshape(b, nh, s, dh)


def mlstm_chunkwise(q, k, v, i_pre, f_pre):
    b, nh, s, dh = q.shape
    nc = s // CHUNK
    q = q * (dh ** -0.5)
    log_f = jax.nn.log_sigmoid(f_pre)
    chunks = lambda t: jnp.moveaxis(t.reshape(b, nh, nc, CHUNK, *t.shape[3:]), 2, 0)
    causal = jnp.tril(jnp.ones((CHUNK, CHUNK), dtype=bool))

    def step(carry, inp):
        C, n, m = carry
        qc, kc, vc, ic, lfc = inp
        bcum = jnp.cumsum(lfc, axis=-1)
        log_d = jnp.where(causal, bcum[..., :, None] - bcum[..., None, :] + ic[..., None, :], -jnp.inf)
        inter_log = bcum + m[..., None]
        m_t = jnp.maximum(jnp.max(log_d, axis=-1), inter_log)
        w_d = jnp.exp(log_d - m_t[..., None])
        w_inter = jnp.exp(inter_log - m_t)
        sc = jnp.einsum('bhld,bhmd->bhlm', qc, kc) * w_d
        num = jnp.einsum('bhlm,bhmd->bhld', sc, vc) + w_inter[..., None] * jnp.einsum('bhld,bhde->bhle', qc, C)
        den = jnp.sum(sc, axis=-1) + w_inter * jnp.einsum('bhld,bhd->bhl', qc, n)
        h = num / jnp.maximum(jnp.abs(den), jnp.exp(-m_t))[..., None]
        b_last = bcum[..., -1]
        g = b_last[..., None] - bcum + ic
        m_new = jnp.maximum(b_last + m, jnp.max(g, axis=-1))
        w_c = jnp.exp(b_last + m - m_new)
        w_g = jnp.exp(g - m_new[..., None])
        C_new = w_c[..., None, None] * C + jnp.einsum('bhld,bhle->bhde', kc * w_g[..., None], vc)
        n_new = w_c[..., None] * n + jnp.einsum('bhl,bhld->bhd', w_g, kc)
        return (C_new, n_new, m_new), h

    init = (jnp.zeros((b, nh, dh, dh), jnp.float32), jnp.zeros((b, nh, dh), jnp.float32),
            jnp.zeros((b, nh), jnp.float32))
    _, hs = lax.scan(step, init, (chunks(q), chunks(k), chunks(v), chunks(i_pre), chunks(log_f)))
    return jnp.moveaxis(hs, 0, 2).reshape(b, nh, s, dh)


def setup_inputs(seed: int = 0) -> dict:
    key = jax.random.key(seed)
    ks = jax.random.split(key, 20)
    f32 = jnp.float32
    nrm = lambda k, shape, fan_in: jax.random.normal(k, shape, f32) * (fan_in ** -0.5)
    x = jax.random.normal(ks[0], (BATCH, SEQ, D_MODEL), f32)
    c = jax.random.normal(ks[1], (BATCH, D_MODEL), f32)
    offsets = jax.random.randint(ks[2], (BATCH, 1), 0, SEQ, dtype=jnp.int32)
    positions = offsets + jnp.arange(SEQ, dtype=jnp.int32)[None, :]
    norm_g = 1.0 + 0.02 * jax.random.normal(ks[3], (DEPTH, N_SUBLAYERS, D_MODEL), f32)
    w_ada = nrm(ks[4], (DEPTH, N_SUBLAYERS, D_MODEL, 3 * D_MODEL), D_MODEL)
    b_ada = 0.02 * jax.random.normal(ks[5], (DEPTH, N_SUBLAYERS, 3 * D_MODEL), f32)
    w_ff1 = nrm(ks[6], (DEPTH, 2, D_MODEL, D_FF), D_MODEL)
    w_ff3 = nrm(ks[7], (DEPTH, 2, D_MODEL, D_FF), D_MODEL)
    w_ff2 = nrm(ks[8], (DEPTH, 2, D_FF, D_MODEL), D_FF)
    w_in = nrm(ks[9], (DEPTH, D_MODEL, D_IN), D_MODEL)
    conv_w = nrm(ks[10], (DEPTH, CONV_WIDTH, 2 * D_MLSTM), CONV_WIDTH)
    conv_b = 0.02 * jax.random.normal(ks[11], (DEPTH, 2 * D_MLSTM), f32)
    b_igate = 0.1 * jax.random.normal(ks[12], (DEPTH, MLSTM_HEADS), f32)
    b_fgate = (jnp.linspace(3.0, 6.0, MLSTM_HEADS, dtype=f32)[None, :]
               + 0.1 * jax.random.normal(ks[13], (DEPTH, MLSTM_HEADS), f32))
    g_ret_norm = 1.0 + 0.02 * jax.random.normal(ks[14], (DEPTH, D_RET), f32)
    g_mlstm_norm = 1.0 + 0.02 * jax.random.normal(ks[15], (DEPTH, D_MLSTM), f32)
    w_out = nrm(ks[16], (DEPTH, D_MIX, D_MODEL), D_MIX)
    g_final = 1.0 + 0.02 * jax.random.normal(ks[17], (D_MODEL,), f32)
    return {"x": x, "c": c, "positions": positions, "norm_g": norm_g, "w_ada": w_ada,
            "b_ada": b_ada, "w_ff1": w_ff1, "w_ff3": w_ff3, "w_ff2": w_ff2, "w_in": w_in,
            "conv_w": conv_w, "conv_b": conv_b, "b_igate": b_igate, "b_fgate": b_fgate,
            "g_ret_norm": g_ret_norm, "g_mlstm_norm": g_mlstm_norm, "w_out": w_out,
            "g_final": g_final}


def reference(x, c, positions, norm_g, w_ada, b_ada, w_ff1, w_ff3, w_ff2, w_in, conv_w, conv_b,
              b_igate, b_fgate, g_ret_norm, g_mlstm_norm, w_out, g_final):
    for l in range(DEPTH):
        shift, scale, gate = ada_mod(c, w_ada[l, 0], b_ada[l, 0])
        h = rmsnorm(x, norm_g[l, 0]) * (1.0 + scale) + shift
        x = x + 0.5 * gate * swiglu(h, w_ff1[l, 0], w_ff3[l, 0], w_ff2[l, 0])

        shift, scale, gate = ada_mod(c, w_ada[l, 1], b_ada[l, 1])
        h = rmsnorm(x, norm_g[l, 1]) * (1.0 + scale) + shift
        z = h @ w_in[l]
        o = 0
        r_q, r_k, r_v, r_g = (z[..., o + j * D_RET:o + (j + 1) * D_RET] for j in range(4))
        o = 4 * D_RET
        m_qk = z[..., o:o + 2 * D_MLSTM]
        m_v = z[..., o + 2 * D_MLSTM:o + 3 * D_MLSTM]
        m_o = z[..., o + 3 * D_MLSTM:o + 4 * D_MLSTM]
        o = o + 4 * D_MLSTM
        m_i = z[..., o:o + MLSTM_HEADS] + b_igate[l]
        m_f = z[..., o + MLSTM_HEADS:o + 2 * MLSTM_HEADS] + b_fgate[l]

        rq = rotary(to_heads(r_q, RET_HEADS), positions)
        rk = rotary(to_heads(r_k, RET_HEADS), positions)
        ret = retention_chunkwise(rq, rk, to_heads(r_v, RET_HEADS))
        y_ret = jax.nn.silu(r_g.astype(jnp.float32)) * head_norm(ret, g_ret_norm[l])

        qk = jax.nn.silu(causal_depthwise_conv(m_qk, conv_w[l], conv_b[l]))
        mq = to_heads(qk[..., :D_MLSTM], MLSTM_HEADS)
        mk = to_heads(qk[..., D_MLSTM:], MLSTM_HEADS)
        i_pre = m_i.astype(jnp.float32).transpose(0, 2, 1)
        f_pre = m_f.astype(jnp.float32).transpose(0, 2, 1)
        mh = mlstm_chunkwise(mq, mk, to_heads(m_v, MLSTM_HEADS), i_pre, f_pre)
        mh = jax.nn.sigmoid(to_heads(m_o, MLSTM_HEADS)) * mh
        y_ml = head_norm(mh, g_mlstm_norm[l])

        y = jnp.concatenate([y_ret, y_ml], axis=-1).astype(x.dtype)
        x = x + gate * (y @ w_out[l])

        shift, scale, gate = ada_mod(c, w_ada[l, 2], b_ada[l, 2])
        h = rmsnorm(x, norm_g[l, 2]) * (1.0 + scale) + shift
        x = x + 0.5 * gate * swiglu(h, w_ff1[l, 1], w_ff3[l, 1], w_ff2[l, 1])
    return rmsnorm(x, g_final)
```

```python
import functools

import jax
import jax.numpy as jnp
from jax import lax
from jax.experimental import pallas as pl
from jax.experimental.pallas import tpu as pltpu

F32 = jnp.float32
BF16 = jnp.bfloat16

RET_HEADS = 4
MLSTM_HEADS = 4
HEAD_DIM = 128
CHUNK = 128
CONV_WIDTH = 4
ROPE_BASE = 10000.0
EPS = 1e-6
N_SUBLAYERS = 3
D_RET = RET_HEADS * HEAD_DIM
D_MLSTM = MLSTM_HEADS * HEAD_DIM

LANES = 128
SUBLANES = 8
MXU_COLS = 256
VMEM_LIMIT_BYTES = 56 * 1024 * 1024

COL_RQ = 0
COL_RK = D_RET
COL_RV = 2 * D_RET
COL_RG = 3 * D_RET
COL_MQK = 4 * D_RET
COL_MV = COL_MQK + 2 * D_MLSTM
COL_MO = COL_MV + D_MLSTM
COL_GATES = COL_MO + D_MLSTM
D_IN_PADDED = COL_GATES + LANES

NT_DIMS = (((1,), (1,)), ((), ()))


def _tiles(seq):
    ffn_rows = min(512, seq)
    mix_rows = min(256, seq)
    assert seq % ffn_rows == 0 and seq % mix_rows == 0 and mix_rows % CHUNK == 0
    return ffn_rows, mix_rows


def _dot(a, b):
    return jnp.dot(a, b, preferred_element_type=F32)


def _rmsnorm(x, g):
    return x * lax.rsqrt(jnp.mean(x * x, axis=-1, keepdims=True) + EPS) * g


def _modulated_norm(x, mod, g):
    shift, scale = mod[0:1], mod[1:2]
    return _rmsnorm(x, g) * (1.0 + scale) + shift


def _head_norm(h, g):
    mu = jnp.mean(h, axis=-1, keepdims=True)
    d = h - mu
    var = jnp.mean(d * d, axis=-1, keepdims=True)
    return d * lax.rsqrt(var + EPS) * g


def _log_sigmoid(x):
    return jnp.minimum(x, 0.0) - jnp.log1p(jnp.exp(-jnp.abs(x)))


def _ada_kernel(c_ref, w_ref, b_ref, o_ref):
    sc = jax.nn.silu(c_ref[...])
    o_ref[...] = jnp.dot(sc, w_ref[...], preferred_element_type=F32,
                         precision=lax.Precision.HIGHEST) + b_ref[...]


def _ada_mod(c, w_ada, b_ada):
    depth, nsub, d, d3 = w_ada.shape
    n = depth * nsub
    b = c.shape[0]
    out = pl.pallas_call(
        _ada_kernel,
        grid=(n, d3 // d),
        in_specs=[
            pl.BlockSpec((b, d), lambda i, j: (0, 0)),
            pl.BlockSpec((None, d, d), lambda i, j: (i, 0, j)),
            pl.BlockSpec((None, 1, d), lambda i, j: (i, 0, j)),
        ],
        out_specs=pl.BlockSpec((None, b, d), lambda i, j: (i, 0, j)),
        out_shape=jax.ShapeDtypeStruct((n, b, d3), F32),
        compiler_params=pltpu.CompilerParams(
            dimension_semantics=("arbitrary", "arbitrary"), vmem_limit_bytes=VMEM_LIMIT_BYTES),
        name="ada_mod",
    )(c, w_ada.reshape(n, d, d3), b_ada.reshape(n, 1, d3))
    return out.reshape(depth, nsub, b, d3 // d, d)


def _rope_kernel(pos_ref, invf_ref, sign_ref, cos_ref, sin_ref):
    ang = pos_ref[...].astype(F32) * invf_ref[...]
    cos_ref[...] = jnp.cos(ang)
    sin_ref[...] = jnp.sin(ang) * sign_ref[...]


def _rope_tables(positions, rows):
    b, s = positions.shape
    half = HEAD_DIM // 2
    inv_freq = ROPE_BASE ** (-jnp.arange(0, HEAD_DIM, 2, dtype=F32) / HEAD_DIM)
    invf = jnp.concatenate([inv_freq, inv_freq]).reshape(1, HEAD_DIM)
    sign = jnp.concatenate([-jnp.ones((half,), F32), jnp.ones((half,), F32)]).reshape(1, HEAD_DIM)
    table = jax.ShapeDtypeStruct((b, s, HEAD_DIM), F32)
    const = pl.BlockSpec((1, HEAD_DIM), lambda i, j: (0, 0))
    tile = pl.BlockSpec((None, rows, HEAD_DIM), lambda i, j: (i, j, 0))
    return pl.pallas_call(
        _rope_kernel,
        grid=(b, s // rows),
        in_specs=[pl.BlockSpec((None, rows, 1), lambda i, j: (i, j, 0)), const, const],
        out_specs=[tile, tile],
        out_shape=[table, table],
        compiler_params=pltpu.CompilerParams(
            dimension_semantics=("arbitrary", "arbitrary"), vmem_limit_bytes=VMEM_LIMIT_BYTES),
        name="rope_tables",
    )(positions.reshape(b, s, 1), invf, sign)


def _ffn_kernel(x_ref, mod_ref, g_ref, w1_ref, w3_ref, w2_ref, *rest, final_norm):
    if final_norm:
        gf_ref, o_ref, act_ref = rest
    else:
        o_ref, act_ref = rest
    x = x_ref[...]
    mod = mod_ref[...]
    h = _modulated_norm(x, mod, g_ref[...]).astype(BF16)
    d_ff = w1_ref.shape[1]
    for c0 in range(0, d_ff, MXU_COLS):
        a = _dot(h, w1_ref[:, c0:c0 + MXU_COLS])
        b = _dot(h, w3_ref[:, c0:c0 + MXU_COLS])
        act_ref[:, c0:c0 + MXU_COLS] = (jax.nn.silu(a) * b).astype(BF16)
    out = x + 0.5 * mod[2:3] * _dot(act_ref[...], w2_ref[...])
    if final_norm:
        out = _rmsnorm(out, gf_ref[...])
    o_ref[...] = out


def _ffn(x, mod, g, w1, w3, w2, rows, g_final=None):
    b, s, d = x.shape
    d_ff = w1.shape[1]
    assert d_ff % MXU_COLS == 0
    final_norm = g_final is not None
    resident = lambda shape: pl.BlockSpec(shape, lambda i, j: (0, 0), pipeline_mode=pl.Buffered(1))
    tile = pl.BlockSpec((None, rows, d), lambda i, j: (i, j, 0))
    in_specs = [
        tile,
        pl.BlockSpec((None, 3, d), lambda i, j: (i, 0, 0)),
        resident((1, d)),
        resident((d, d_ff)),
        resident((d, d_ff)),
        resident((d_ff, d)),
    ]
    args = [x, mod, g.reshape(1, d), w1.astype(BF16), w3.astype(BF16), w2.astype(BF16)]
    if final_norm:
        in_specs.append(resident((1, d)))
        args.append(g_final.reshape(1, d))
    return pl.pallas_call(
        functools.partial(_ffn_kernel, final_norm=final_norm),
        grid=(b, s // rows),
        in_specs=in_specs,
        out_specs=tile,
        out_shape=jax.ShapeDtypeStruct(x.shape, x.dtype),
        scratch_shapes=[pltpu.VMEM((rows, d_ff), BF16)],
        compiler_params=pltpu.CompilerParams(
            dimension_semantics=("arbitrary", "arbitrary"), vmem_limit_bytes=VMEM_LIMIT_BYTES),
        name="ffn_final" if final_norm else "ffn",
    )(*args)


def _retention_head(hh, rows, cos, sin, z_ref, decay_ref, wq_ref, wk_ref, cdec_ref, gret_ref,
                    state_ref, y_ref):
    c0 = hh * HEAD_DIM
    q = z_ref[rows, COL_RQ + c0:COL_RQ + c0 + HEAD_DIM]
    k = z_ref[rows, COL_RK + c0:COL_RK + c0 + HEAD_DIM]
    v = z_ref[rows, COL_RV + c0:COL_RV + c0 + HEAD_DIM]
    rg = z_ref[rows, COL_RG + c0:COL_RG + c0 + HEAD_DIM]
    half = HEAD_DIM // 2
    qr = q * cos + pltpu.roll(q, half, 1) * sin
    kr = (k * cos + pltpu.roll(k, half, 1) * sin) * (HEAD_DIM ** -0.5)
    vb = v.astype(BF16)
    scores = lax.dot_general(qr.astype(BF16), kr.astype(BF16), NT_DIMS,
                             preferred_element_type=F32) * decay_ref[hh]
    state = state_ref[hh]
    o = _dot(scores.astype(BF16), vb) + _dot((qr * wq_ref[hh]).astype(BF16), state.astype(BF16))
    kw = kr * wk_ref[hh]
    state_ref[hh] = state * cdec_ref[hh] + _dot(kw.T.astype(BF16), vb)
    hn = _head_norm(o, gret_ref[:, c0:c0 + HEAD_DIM])
    y_ref[rows, c0:c0 + HEAD_DIM] = (jax.nn.silu(rg) * hn).astype(y_ref.dtype)


def _mlstm_head(hh, rows, causal, gates, gates_t, bcum, bcum_t, z_ref, gml_ref,
                c_ref, n_ref, m_ref, y_ref):
    c0 = hh * HEAD_DIM
    q = z_ref[rows, COL_MQK + c0:COL_MQK + c0 + HEAD_DIM] * (HEAD_DIM ** -0.5)
    k = z_ref[rows, COL_MQK + D_MLSTM + c0:COL_MQK + D_MLSTM + c0 + HEAD_DIM]
    v = z_ref[rows, COL_MV + c0:COL_MV + c0 + HEAD_DIM]
    og = z_ref[rows, COL_MO + c0:COL_MO + c0 + HEAD_DIM]
    qb, kb, vb = q.astype(BF16), k.astype(BF16), v.astype(BF16)
    fcol = MLSTM_HEADS + hh
    b_col = bcum[:, fcol:fcol + 1]
    b_row = bcum_t[fcol:fcol + 1, :]
    i_col = gates[:, hh:hh + 1]
    i_row = gates_t[hh:hh + 1, :]
    m_prev = m_ref[hh][:, 0:1]
    c_prev = c_ref[hh]
    n_prev = n_ref[hh]

    log_d = jnp.where(causal, b_col - b_row + i_row, -jnp.inf)
    inter_log = b_col + m_prev
    m_t = jnp.maximum(jnp.max(log_d, axis=-1, keepdims=True), inter_log)
    w_d = jnp.exp(log_d - m_t)
    w_inter = jnp.exp(inter_log - m_t)
    sc = lax.dot_general(qb, kb, NT_DIMS, preferred_element_type=F32) * w_d
    num = _dot(sc.astype(BF16), vb) + w_inter * _dot(qb, c_prev.astype(BF16))
    den = (jnp.sum(sc, axis=-1, keepdims=True)
           + w_inter * jnp.sum(q * n_prev, axis=-1, keepdims=True))
    h = num / jnp.maximum(jnp.abs(den), jnp.exp(-m_t))

    b_last = b_col[CHUNK - 1:CHUNK, :]
    g = b_last - b_col + i_col
    m_new = jnp.maximum(b_last + m_prev, jnp.max(g, axis=0, keepdims=True))
    w_c = jnp.exp(b_last + m_prev - m_new)
    kw = k * jnp.exp(g - m_new)
    c_ref[hh] = w_c * c_prev + _dot(kw.T.astype(BF16), vb)
    n_ref[hh] = w_c * n_prev + jnp.sum(kw, axis=0, keepdims=True)
    m_ref[hh] = jnp.broadcast_to(m_new, (1, LANES))

    mh = jax.nn.sigmoid(og) * h
    y_ref[rows, D_RET + c0:D_RET + c0 + HEAD_DIM] = _head_norm(
        mh, gml_ref[:, c0:c0 + HEAD_DIM]).astype(y_ref.dtype)


def _mixer_kernel(x_ref, mod_ref, g_ref, win_ref, gbias_ref, cos_ref, sin_ref, convw_ref, convb_ref,
                  decay_ref, wq_ref, wk_ref, cdec_ref, tril_ref, gret_ref, gml_ref, wout_ref,
                  o_ref,
                  z_ref, qkraw_ref, gate_ref, y_ref, sret_ref, c_ref, n_ref, m_ref):
    rows_total = x_ref.shape[0]

    @pl.when(pl.program_id(1) == 0)
    def _start_of_sequence():
        sret_ref[...] = jnp.zeros_like(sret_ref)
        c_ref[...] = jnp.zeros_like(c_ref)
        n_ref[...] = jnp.zeros_like(n_ref)
        m_ref[...] = jnp.zeros_like(m_ref)
        qkraw_ref[0:SUBLANES, :] = jnp.zeros((SUBLANES, qkraw_ref.shape[1]), F32)

    x = x_ref[...]
    mod = mod_ref[...]
    h = _modulated_norm(x, mod, g_ref[...]).astype(BF16)

    group = D_RET
    for c0 in range(0, COL_GATES, group):
        zc = _dot(h, win_ref[:, c0:c0 + group])
        if COL_MQK <= c0 < COL_MV:
            qkraw_ref[SUBLANES:SUBLANES + rows_total, c0 - COL_MQK:c0 - COL_MQK + group] = zc
        else:
            z_ref[:, c0:c0 + group] = zc
    zg = _dot(h, win_ref[:, COL_GATES:COL_GATES + LANES]) + gbias_ref[...]
    lane = lax.broadcasted_iota(jnp.int32, zg.shape, 1)
    gate_ref[...] = jnp.where(lane < MLSTM_HEADS, zg, _log_sigmoid(zg))

    for r0 in range(0, rows_total, CHUNK):
        for c0 in range(0, 2 * D_MLSTM, MXU_COLS):
            acc = convb_ref[:, c0:c0 + MXU_COLS]
            for j in range(CONV_WIDTH):
                start = SUBLANES - (CONV_WIDTH - 1) + j + r0
                acc = acc + (convw_ref[j:j + 1, c0:c0 + MXU_COLS]
                             * qkraw_ref[start:start + CHUNK, c0:c0 + MXU_COLS])
            z_ref[r0:r0 + CHUNK, COL_MQK + c0:COL_MQK + c0 + MXU_COLS] = jax.nn.silu(acc)
    qkraw_ref[0:SUBLANES, :] = qkraw_ref[rows_total:rows_total + SUBLANES, :]

    row_id = lax.broadcasted_iota(jnp.int32, (CHUNK, CHUNK), 0)
    col_id = lax.broadcasted_iota(jnp.int32, (CHUNK, CHUNK), 1)
    causal = row_id >= col_id

    def chunk_body(ci, carry):
        rows = pl.ds(pl.multiple_of(ci * CHUNK, CHUNK), CHUNK)
        cos = cos_ref[rows, :]
        sin = sin_ref[rows, :]
        for hh in range(RET_HEADS):
            _retention_head(hh, rows, cos, sin, z_ref, decay_ref, wq_ref, wk_ref, cdec_ref,
                            gret_ref, sret_ref, y_ref)
        gates = gate_ref[rows, :]
        bcum = jnp.dot(tril_ref[...], gates, preferred_element_type=F32,
                       precision=lax.Precision.HIGHEST)
        gates_t = gates.T
        bcum_t = bcum.T
        for hh in range(MLSTM_HEADS):
            _mlstm_head(hh, rows, causal, gates, gates_t, bcum, bcum_t, z_ref, gml_ref,
                        c_ref, n_ref, m_ref, y_ref)
        return carry

    lax.fori_loop(0, rows_total // CHUNK, chunk_body, 0)

    o_ref[...] = x + mod[2:3] * _dot(y_ref[...], wout_ref[...])


def _retention_constants():
    heads = jnp.arange(RET_HEADS, dtype=F32)
    log_gamma = jnp.log(1.0 - 2.0 ** (-5.0 - heads))
    idx = jnp.arange(CHUNK)
    diff = (idx[:, None] - idx[None, :]).astype(F32)
    decay = jnp.where(diff >= 0, jnp.exp(log_gamma[:, None, None] * jnp.maximum(diff, 0.0)), 0.0)
    w_k = jnp.exp(log_gamma[:, None] * (CHUNK - 1 - idx).astype(F32))
    w_q = jnp.exp(log_gamma[:, None] * (idx + 1).astype(F32))
    chunk_decay = jnp.exp(log_gamma * CHUNK)
    bcast = lambda t: jnp.broadcast_to(t[:, :, None], (RET_HEADS, CHUNK, HEAD_DIM))
    cdec = jnp.broadcast_to(chunk_decay[:, None, None], (RET_HEADS, 1, HEAD_DIM))
    return decay, bcast(w_q), bcast(w_k), cdec


def _mixer(x, mod, g, w_in, conv_w, conv_b, b_igate, b_fgate, g_ret, g_ml, w_out, cos, sin, rows):
    b, s, d = x.shape
    d_mix = D_RET + D_MLSTM
    n_gates = 2 * MLSTM_HEADS
    assert w_in.shape == (d, COL_GATES + n_gates) and w_out.shape == (d_mix, d)
    win = jnp.pad(w_in, ((0, 0), (0, D_IN_PADDED - w_in.shape[1]))).astype(BF16)
    gbias = jnp.pad(jnp.concatenate([b_igate, b_fgate]), (0, LANES - n_gates)).reshape(1, LANES)
    decay, w_q, w_k, cdec = _retention_constants()
    tril = jnp.tril(jnp.ones((CHUNK, CHUNK), F32))

    def resident(shape):
        zeros = (0,) * len(shape)
        return pl.BlockSpec(shape, lambda i, j: zeros, pipeline_mode=pl.Buffered(1))

    tile = lambda width: pl.BlockSpec((None, rows, width), lambda i, j: (i, j, 0))
    in_specs = [
        tile(d),
        pl.BlockSpec((None, 3, d), lambda i, j: (i, 0, 0)),
        resident((1, d)),
        resident((d, D_IN_PADDED)),
        resident((1, LANES)),
        tile(HEAD_DIM),
        tile(HEAD_DIM),
        resident((CONV_WIDTH, 2 * D_MLSTM)),
        resident((1, 2 * D_MLSTM)),
        resident((RET_HEADS, CHUNK, CHUNK)),
        resident((RET_HEADS, CHUNK, HEAD_DIM)),
        resident((RET_HEADS, CHUNK, HEAD_DIM)),
        resident((RET_HEADS, 1, HEAD_DIM)),
        resident((CHUNK, CHUNK)),
        resident((1, D_RET)),
        resident((1, D_MLSTM)),
        resident((d_mix, d)),
    ]
    scratch_shapes = [
        pltpu.VMEM((rows, COL_GATES), F32),
        pltpu.VMEM((rows + SUBLANES, 2 * D_MLSTM), F32),
        pltpu.VMEM((rows, LANES), F32),
        pltpu.VMEM((rows, d_mix), BF16),
        pltpu.VMEM((RET_HEADS, HEAD_DIM, HEAD_DIM), F32),
        pltpu.VMEM((MLSTM_HEADS, HEAD_DIM, HEAD_DIM), F32),
        pltpu.VMEM((MLSTM_HEADS, 1, HEAD_DIM), F32),
        pltpu.VMEM((MLSTM_HEADS, 1, LANES), F32),
    ]
    return pl.pallas_call(
        _mixer_kernel,
        grid=(b, s // rows),
        in_specs=in_specs,
        out_specs=tile(d),
        out_shape=jax.ShapeDtypeStruct(x.shape, x.dtype),
        scratch_shapes=scratch_shapes,
        compiler_params=pltpu.CompilerParams(
            dimension_semantics=("arbitrary", "arbitrary"), vmem_limit_bytes=VMEM_LIMIT_BYTES),
        name="mixer",
    )(x, mod, g.reshape(1, d), win, gbias, cos, sin, conv_w, conv_b.reshape(1, -1),
      decay, w_q, w_k, cdec, tril, g_ret.reshape(1, -1), g_ml.reshape(1, -1), w_out.astype(BF16))


@jax.jit
def kernel(x, c, positions, norm_g, w_ada, b_ada, w_ff1, w_ff3, w_ff2, w_in, conv_w, conv_b,
           b_igate, b_fgate, g_ret_norm, g_mlstm_norm, w_out, g_final):
    depth = w_in.shape[0]
    ffn_rows, mix_rows = _tiles(x.shape[1])
    mods = _ada_mod(c, w_ada, b_ada)
    cos, sin = _rope_tables(positions, mix_rows)
    for l in range(depth):
        x = _ffn(x, mods[l, 0], norm_g[l, 0], w_ff1[l, 0], w_ff3[l, 0], w_ff2[l, 0], ffn_rows)
        x = _mixer(x, mods[l, 1], norm_g[l, 1], w_in[l], conv_w[l], conv_b[l], b_igate[l],
                   b_fgate[l], g_ret_norm[l], g_mlstm_norm[l], w_out[l], cos, sin, mix_rows)
        x = _ffn(x, mods[l, 2], norm_g[l, 2], w_ff1[l, 1], w_ff3[l, 1], w_ff2[l, 1], ffn_rows,
                 g_final=g_final if l == depth - 1 else None)
    return x
```

```python
import functools

import jax
import jax.numpy as jnp
from jax import lax
from jax.experimental import pallas as pl
from jax.experimental.pallas import tpu as pltpu

F32 = jnp.float32
BF16 = jnp.bfloat16

RET_HEADS = 4
MLSTM_HEADS = 4
HEAD_DIM = 128
CHUNK = 128
CONV_WIDTH = 4
ROPE_BASE = 10000.0
EPS = 1e-6
N_SUBLAYERS = 3
D_RET = RET_HEADS * HEAD_DIM
D_MLSTM = MLSTM_HEADS * HEAD_DIM

LANES = 128
SUBLANES = 8
MXU_COLS = 256
VMEM_LIMIT_BYTES = 56 * 1024 * 1024

COL_RQ = 0
COL_RK = D_RET
COL_RV = 2 * D_RET
COL_RG = 3 * D_RET
COL_MQK = 4 * D_RET
COL_MV = COL_MQK + 2 * D_MLSTM
COL_MO = COL_MV + D_MLSTM
COL_GATES = COL_MO + D_MLSTM
D_IN_PADDED = COL_GATES + LANES

NT_DIMS = (((1,), (1,)), ((), ()))


def _tiles(seq):
    ffn_rows = min(512, seq)
    mix_rows = min(512, seq)
    assert seq % ffn_rows == 0 and seq % mix_rows == 0 and mix_rows % CHUNK == 0
    return ffn_rows, mix_rows


def _dot(a, b):
    return jnp.dot(a, b, preferred_element_type=F32)


def _rmsnorm(x, g):
    return x * lax.rsqrt(jnp.mean(x * x, axis=-1, keepdims=True) + EPS) * g


def _modulated_norm(x, mod, g):
    shift, scale = mod[0:1], mod[1:2]
    return _rmsnorm(x, g) * (1.0 + scale) + shift


def _log_sigmoid(x):
    return jnp.minimum(x, 0.0) - jnp.log1p(jnp.exp(-jnp.abs(x)))


def _ada_kernel(c_ref, w_ref, b_ref, o_ref):
    sc = jax.nn.silu(c_ref[...])
    o_ref[...] = jnp.dot(sc, w_ref[...], preferred_element_type=F32,
                         precision=lax.Precision.HIGHEST) + b_ref[...]


def _ada_mod(c, w_ada, b_ada):
    depth, nsub, d, d3 = w_ada.shape
    n = depth * nsub
    b = c.shape[0]
    out = pl.pallas_call(
        _ada_kernel,
        grid=(n, d3 // d),
        in_specs=[
            pl.BlockSpec((b, d), lambda i, j: (0, 0)),
            pl.BlockSpec((None, d, d), lambda i, j: (i, 0, j)),
            pl.BlockSpec((None, 1, d), lambda i, j: (i, 0, j)),
        ],
        out_specs=pl.BlockSpec((None, b, d), lambda i, j: (i, 0, j)),
        out_shape=jax.ShapeDtypeStruct((n, b, d3), F32),
        compiler_params=pltpu.CompilerParams(
            dimension_semantics=("arbitrary", "arbitrary"), vmem_limit_bytes=VMEM_LIMIT_BYTES),
        name="ada_mod",
    )(c, w_ada.reshape(n, d, d3), b_ada.reshape(n, 1, d3))
    return out.reshape(depth, nsub, b, d3 // d, d)


def _rope_kernel(pos_ref, invf_ref, sign_ref, cos_ref, sin_ref):
    ang = pos_ref[...].astype(F32) * invf_ref[...]
    cos_ref[...] = jnp.cos(ang)
    sin_ref[...] = jnp.sin(ang) * sign_ref[...]


def _rope_tables(positions, rows):
    b, s = positions.shape
    half = HEAD_DIM // 2
    inv_freq = ROPE_BASE ** (-jnp.arange(0, HEAD_DIM, 2, dtype=F32) / HEAD_DIM)
    invf = jnp.concatenate([inv_freq, inv_freq]).reshape(1, HEAD_DIM)
    sign = jnp.concatenate([-jnp.ones((half,), F32), jnp.ones((half,), F32)]).reshape(1, HEAD_DIM)
    table = jax.ShapeDtypeStruct((b, s, HEAD_DIM), F32)
    const = pl.BlockSpec((1, HEAD_DIM), lambda i, j: (0, 0))
    tile = pl.BlockSpec((None, rows, HEAD_DIM), lambda i, j: (i, j, 0))
    return pl.pallas_call(
        _rope_kernel,
        grid=(b, s // rows),
        in_specs=[pl.BlockSpec((None, rows, 1), lambda i, j: (i, j, 0)), const, const],
        out_specs=[tile, tile],
        out_shape=[table, table],
        compiler_params=pltpu.CompilerParams(
            dimension_semantics=("arbitrary", "arbitrary"), vmem_limit_bytes=VMEM_LIMIT_BYTES),
        name="rope_tables",
    )(positions.reshape(b, s, 1), invf, sign)


def _ffn_kernel(x_ref, mod_ref, g_ref, w1_ref, w3_ref, w2_ref, *rest, final_norm):
    if final_norm:
        gf_ref, o_ref, act_ref = rest
    else:
        o_ref, act_ref = rest
    x = x_ref[...]
    mod = mod_ref[...]
    h = _modulated_norm(x, mod, g_ref[...]).astype(BF16)
    d_ff = w1_ref.shape[1]
    for c0 in range(0, d_ff, MXU_COLS):
        a = _dot(h, w1_ref[:, c0:c0 + MXU_COLS])
        b = _dot(h, w3_ref[:, c0:c0 + MXU_COLS])
        act_ref[:, c0:c0 + MXU_COLS] = (jax.nn.silu(a) * b).astype(BF16)
    out = x + 0.5 * mod[2:3] * _dot(act_ref[...], w2_ref[...])
    if final_norm:
        out = _rmsnorm(out, gf_ref[...])
    o_ref[...] = out


def _ffn(x, mod, g, w1, w3, w2, rows, g_final=None):
    b, s, d = x.shape
    d_ff = w1.shape[1]
    assert d_ff % MXU_COLS == 0
    final_norm = g_final is not None
    resident = lambda shape: pl.BlockSpec(shape, lambda i, j: (0, 0), pipeline_mode=pl.Buffered(1))
    tile = pl.BlockSpec((None, rows, d), lambda i, j: (i, j, 0))
    in_specs = [
        tile,
        pl.BlockSpec((None, 3, d), lambda i, j: (i, 0, 0)),
        resident((1, d)),
        resident((d, d_ff)),
        resident((d, d_ff)),
        resident((d_ff, d)),
    ]
    args = [x, mod, g.reshape(1, d), w1.astype(BF16), w3.astype(BF16), w2.astype(BF16)]
    if final_norm:
        in_specs.append(resident((1, d)))
        args.append(g_final.reshape(1, d))
    return pl.pallas_call(
        functools.partial(_ffn_kernel, final_norm=final_norm),
        grid=(b, s // rows),
        in_specs=in_specs,
        out_specs=tile,
        out_shape=jax.ShapeDtypeStruct(x.shape, x.dtype),
        scratch_shapes=[pltpu.VMEM((rows, d_ff), BF16)],
        compiler_params=pltpu.CompilerParams(
            dimension_semantics=("arbitrary", "arbitrary"), vmem_limit_bytes=VMEM_LIMIT_BYTES),
        name="ffn_final" if final_norm else "ffn",
    )(*args)


Z_RQ = 0
Z_RK = D_RET
Z_RG = 2 * D_RET
Z_MO = 3 * D_RET
Z_WIDTH = 3 * D_RET + D_MLSTM


def _row_mean_lanes(t, avg):
    hi = t.astype(BF16)
    lo = (t - hi.astype(F32)).astype(BF16)
    return _dot(hi, avg) + _dot(lo, avg)


def _prefix_max_rows(t, row_id):
    shift = 1
    while shift < CHUNK:
        t = jnp.maximum(t, jnp.where(row_id >= shift, pltpu.roll(t, shift, 0), -jnp.inf))
        shift *= 2
    return t


def _stage_retention(r, hh, z_ref, cos, sin, wq_ref, wk_ref, rq_ref, rqw_ref, rk_ref, rkwt_ref):
    r0, c0 = r * CHUNK, hh * HEAD_DIM
    q = z_ref[r0:r0 + CHUNK, Z_RQ + c0:Z_RQ + c0 + HEAD_DIM]
    k = z_ref[r0:r0 + CHUNK, Z_RK + c0:Z_RK + c0 + HEAD_DIM]
    half = HEAD_DIM // 2
    qr = q * cos + pltpu.roll(q, half, 1) * sin
    kr = (k * cos + pltpu.roll(k, half, 1) * sin) * (HEAD_DIM ** -0.5)
    rq_ref[r0:r0 + CHUNK, c0:c0 + HEAD_DIM] = qr.astype(BF16)
    rqw_ref[r0:r0 + CHUNK, c0:c0 + HEAD_DIM] = (qr * wq_ref[hh]).astype(BF16)
    rk_ref[r0:r0 + CHUNK, c0:c0 + HEAD_DIM] = kr.astype(BF16)
    rkwt_ref[r, hh] = (kr * wk_ref[hh]).T.astype(BF16)


def _stage_mlstm_qk(r, blk, qkraw_ref, convw_ref, convb_ref, mq_ref, mk_ref, mkt_ref):
    r0, c0 = r * CHUNK, blk * HEAD_DIM
    acc = convb_ref[:, c0:c0 + HEAD_DIM]
    for j in range(CONV_WIDTH):
        start = SUBLANES - (CONV_WIDTH - 1) + j + r0
        acc = acc + (convw_ref[j:j + 1, c0:c0 + HEAD_DIM]
                     * qkraw_ref[start:start + CHUNK, c0:c0 + HEAD_DIM])
    qk = jax.nn.silu(acc)
    if blk < MLSTM_HEADS:
        mq_ref[r0:r0 + CHUNK, c0:c0 + HEAD_DIM] = (qk * (HEAD_DIM ** -0.5)).astype(BF16)
    else:
        hh = blk - MLSTM_HEADS
        mk_ref[r0:r0 + CHUNK, hh * HEAD_DIM:(hh + 1) * HEAD_DIM] = qk.astype(BF16)
        mkt_ref[r, hh] = qk.T


def _stage_gates(r, gate_ref, tril_ref, row_id, cmb_ref, bb_ref, arow_ref):
    r0 = r * CHUNK
    gates = gate_ref[r0:r0 + CHUNK, :]
    bcum = jnp.dot(tril_ref[...], gates, preferred_element_type=F32,
                   precision=lax.Precision.HIGHEST)
    a = gates - pltpu.roll(bcum, LANES - MLSTM_HEADS, 1)
    cmax = _prefix_max_rows(a, row_id)
    arow_ref[r] = a.T[0:SUBLANES, :]
    for hh in range(MLSTM_HEADS):
        cols = slice(hh * HEAD_DIM, (hh + 1) * HEAD_DIM)
        cmb_ref[r0:r0 + CHUNK, cols] = jnp.broadcast_to(cmax[:, hh:hh + 1], (CHUNK, HEAD_DIM))
        fcol = MLSTM_HEADS + hh
        bb_ref[r0:r0 + CHUNK, cols] = jnp.broadcast_to(bcum[:, fcol:fcol + 1], (CHUNK, HEAD_DIM))


def _retention_chunk(hh, r, rq_ref, rqw_ref, rk_ref, rv_ref, rkwt_ref, decay_ref, cdec_ref,
                     state_ref, pre_ref):
    rows = slice(r * CHUNK, (r + 1) * CHUNK)
    cols = slice(hh * HEAD_DIM, (hh + 1) * HEAD_DIM)
    vb = rv_ref[rows, cols]
    scores = lax.dot_general(rq_ref[rows, cols], rk_ref[rows, cols], NT_DIMS,
                             preferred_element_type=F32) * decay_ref[hh]
    state = state_ref[hh]
    pre_ref[rows, cols] = (_dot(scores.astype(BF16), vb)
                           + _dot(rqw_ref[rows, cols], state.astype(BF16)))
    state_ref[hh] = state * cdec_ref[hh] + _dot(rkwt_ref[r, hh], vb)


def _mlstm_chunk(hh, r, causal, z_ref, mq_ref, mk_ref, mv_ref, mkt_ref, cmb_ref, bb_ref, arow_ref,
                 c_ref, m_ref, pre_ref):
    rows = slice(r * CHUNK, (r + 1) * CHUNK)
    c0 = hh * HEAD_DIM
    cols = slice(c0, c0 + HEAD_DIM)
    qb = mq_ref[rows, cols]
    kb = mk_ref[rows, cols]
    v_aug = mv_ref[rows, 2 * c0:2 * c0 + 2 * HEAD_DIM]
    og = z_ref[rows, Z_MO + c0:Z_MO + c0 + HEAD_DIM]
    cmax = cmb_ref[rows, cols]
    b_l = bb_ref[rows, cols]
    a_row = arow_ref[r][hh:hh + 1, :]
    m_prev = m_ref[hh]
    state = c_ref[hh]

    mx = jnp.maximum(cmax, m_prev)
    w_d = jnp.exp(jnp.where(causal, a_row - mx, -jnp.inf))
    w_inter = jnp.exp(m_prev - mx)
    sc = lax.dot_general(qb, kb, NT_DIMS, preferred_element_type=F32) * w_d
    tot = (_dot(sc.astype(BF16), v_aug)
           + jnp.concatenate([w_inter, w_inter], axis=1) * _dot(qb, state.astype(BF16)))
    num, den = tot[:, :HEAD_DIM], tot[:, HEAD_DIM:]
    h = num / jnp.maximum(jnp.abs(den), jnp.exp(-(b_l + mx)))
    pre_ref[rows, D_RET + c0:D_RET + c0 + HEAD_DIM] = jax.nn.sigmoid(og) * h

    b_last = b_l[CHUNK - 1:CHUNK, :]
    m_new = jnp.maximum(b_last + m_prev, b_last + cmax[CHUNK - 1:CHUNK, :])
    w_c = jnp.exp(b_last + m_prev - m_new)
    w_g = jnp.exp(b_last + a_row - m_new)
    kwt = (mkt_ref[r, hh] * w_g).astype(BF16)
    c_ref[hh] = jnp.concatenate([w_c, w_c], axis=1) * state + _dot(kwt, v_aug)
    m_ref[hh] = m_new


def _mixer_kernel(x_ref, mod_ref, g_ref, win_ref, gbias_ref, cos_ref, sin_ref, convw_ref, convb_ref,
                  decay_ref, wq_ref, wk_ref, cdec_ref, tril_ref, avg_ref, gret_ref, gml_ref, wout_ref,
                  o_ref,
                  h_ref, z_ref, qkraw_ref, gate_ref, pre_ref, y_ref,
                  rq_ref, rqw_ref, rk_ref, rv_ref, rkwt_ref, mq_ref, mk_ref, mv_ref, mkt_ref,
                  cmb_ref, bb_ref, arow_ref, sret_ref, c_ref, m_ref):
    rows_total = x_ref.shape[0]
    n_chunks = rows_total // CHUNK

    @pl.when(pl.program_id(1) == 0)
    def _start_of_sequence():
        sret_ref[...] = jnp.zeros_like(sret_ref)
        c_ref[...] = jnp.zeros_like(c_ref)
        m_ref[...] = jnp.zeros_like(m_ref)
        qkraw_ref[0:SUBLANES, :] = jnp.zeros((SUBLANES, qkraw_ref.shape[1]), F32)
        for hh in range(MLSTM_HEADS):
            ones_cols = slice((2 * hh + 1) * HEAD_DIM, (2 * hh + 2) * HEAD_DIM)
            mv_ref[:, ones_cols] = jnp.ones((rows_total, HEAD_DIM), BF16)

    mod = mod_ref[...]
    h_ref[...] = _modulated_norm(x_ref[...], mod, g_ref[...]).astype(BF16)

    def project(c0, width=D_RET):
        return _dot(h_ref[...], win_ref[:, c0:c0 + width])

    row_id = lax.broadcasted_iota(jnp.int32, (CHUNK, CHUNK), 0)
    col_id = lax.broadcasted_iota(jnp.int32, (CHUNK, CHUNK), 1)
    causal = row_id >= col_id

    def project_to_z(col, zcol):
        z_ref[:, zcol:zcol + D_RET] = project(col)

    def project_qk(c0):
        qkraw_ref[SUBLANES:SUBLANES + rows_total, c0:c0 + D_RET] = project(COL_MQK + c0)

    def project_gates():
        zg = project(COL_GATES, LANES) + gbias_ref[...]
        lane = lax.broadcasted_iota(jnp.int32, zg.shape, 1)
        gate_ref[...] = jnp.where(lane < MLSTM_HEADS, zg, _log_sigmoid(zg))

    def project_rv():
        rv_ref[...] = project(COL_RV).astype(BF16)

    def project_mv():
        mv = project(COL_MV)
        for hh in range(MLSTM_HEADS):
            mv_ref[:, 2 * hh * HEAD_DIM:(2 * hh + 1) * HEAD_DIM] = (
                mv[:, hh * HEAD_DIM:(hh + 1) * HEAD_DIM].astype(BF16))

    def stage_retention(r):
        cos = cos_ref[r * CHUNK:(r + 1) * CHUNK, :]
        sin = sin_ref[r * CHUNK:(r + 1) * CHUNK, :]
        for hh in range(RET_HEADS):
            _stage_retention(r, hh, z_ref, cos, sin, wq_ref, wk_ref, rq_ref, rqw_ref, rk_ref,
                             rkwt_ref)

    def stage_mlstm(r):
        for blk in range(2 * MLSTM_HEADS):
            _stage_mlstm_qk(r, blk, qkraw_ref, convw_ref, convb_ref, mq_ref, mk_ref, mkt_ref)
        _stage_gates(r, gate_ref, tril_ref, row_id, cmb_ref, bb_ref, arow_ref)

    projections = [
        functools.partial(project_to_z, COL_RQ, Z_RQ),
        functools.partial(project_to_z, COL_RK, Z_RK),
        functools.partial(project_qk, 0),
        functools.partial(project_qk, D_RET),
        project_gates,
        project_rv,
        functools.partial(project_to_z, COL_RG, Z_RG),
        project_mv,
        functools.partial(project_to_z, COL_MO, Z_MO),
    ]
    ret_ready, mlstm_ready = 1, 5
    stages = ([functools.partial(stage_retention, r) for r in range(n_chunks)]
              + [functools.partial(stage_mlstm, r) for r in range(n_chunks)])
    for i, run_projection in enumerate(projections):
        run_projection()
        ready = n_chunks if i < mlstm_ready else 2 * n_chunks
        if i >= ret_ready and len(stages) > 2 * n_chunks - ready:
            stages.pop(0)()
    for run_stage in stages:
        run_stage()
    qkraw_ref[0:SUBLANES, :] = qkraw_ref[rows_total:rows_total + SUBLANES, :]

    for r in range(n_chunks):
        for hh in range(RET_HEADS):
            _retention_chunk(hh, r, rq_ref, rqw_ref, rk_ref, rv_ref, rkwt_ref, decay_ref, cdec_ref,
                             sret_ref, pre_ref)
        for hh in range(MLSTM_HEADS):
            _mlstm_chunk(hh, r, causal, z_ref, mq_ref, mk_ref, mv_ref, mkt_ref, cmb_ref, bb_ref,
                         arow_ref, c_ref, m_ref, pre_ref)

    pair = 2 * HEAD_DIM
    avg = avg_ref[...]
    for p0 in range(0, D_RET + D_MLSTM, pair):
        pre = pre_ref[:, p0:p0 + pair]
        d = pre - _row_mean_lanes(pre, avg)
        var = _row_mean_lanes(d * d, avg)
        if p0 < D_RET:
            hn = (jax.nn.silu(z_ref[:, Z_RG + p0:Z_RG + p0 + pair])
                  * (d * lax.rsqrt(var + EPS) * gret_ref[:, p0:p0 + pair]))
        else:
            hn = d * lax.rsqrt(var + EPS) * gml_ref[:, p0 - D_RET:p0 - D_RET + pair]
        y_ref[:, p0:p0 + pair] = hn.astype(y_ref.dtype)
    o_ref[...] = x_ref[...] + mod[2:3] * _dot(y_ref[...], wout_ref[...])


def _retention_constants():
    heads = jnp.arange(RET_HEADS, dtype=F32)
    log_gamma = jnp.log(1.0 - 2.0 ** (-5.0 - heads))
    idx = jnp.arange(CHUNK)
    diff = (idx[:, None] - idx[None, :]).astype(F32)
    decay = jnp.where(diff >= 0, jnp.exp(log_gamma[:, None, None] * jnp.maximum(diff, 0.0)), 0.0)
    w_k = jnp.exp(log_gamma[:, None] * (CHUNK - 1 - idx).astype(F32))
    w_q = jnp.exp(log_gamma[:, None] * (idx + 1).astype(F32))
    chunk_decay = jnp.exp(log_gamma * CHUNK)
    bcast = lambda t: jnp.broadcast_to(t[:, :, None], (RET_HEADS, CHUNK, HEAD_DIM))
    cdec = jnp.broadcast_to(chunk_decay[:, None, None], (RET_HEADS, 1, HEAD_DIM))
    return decay, bcast(w_q), bcast(w_k), cdec


def _mixer(x, mod, g, w_in, conv_w, conv_b, b_igate, b_fgate, g_ret, g_ml, w_out, cos, sin, rows):
    b, s, d = x.shape
    d_mix = D_RET + D_MLSTM
    n_gates = 2 * MLSTM_HEADS
    n_chunks = rows // CHUNK
    assert w_in.shape == (d, COL_GATES + n_gates) and w_out.shape == (d_mix, d)
    win = jnp.pad(w_in, ((0, 0), (0, D_IN_PADDED - w_in.shape[1]))).astype(BF16)
    gbias = jnp.pad(jnp.concatenate([b_igate, b_fgate]), (0, LANES - n_gates)).reshape(1, LANES)
    decay, w_q, w_k, cdec = _retention_constants()
    tril = jnp.tril(jnp.ones((CHUNK, CHUNK), F32))
    head_of = jnp.arange(2 * HEAD_DIM) // HEAD_DIM
    avg = jnp.where(head_of[:, None] == head_of[None, :], 1.0 / HEAD_DIM, 0.0).astype(BF16)

    def resident(shape):
        zeros = (0,) * len(shape)
        return pl.BlockSpec(shape, lambda i, j: zeros, pipeline_mode=pl.Buffered(1))

    tile = lambda width: pl.BlockSpec((None, rows, width), lambda i, j: (i, j, 0))
    in_specs = [
        tile(d),
        pl.BlockSpec((None, 3, d), lambda i, j: (i, 0, 0)),
        resident((1, d)),
        resident((d, D_IN_PADDED)),
        resident((1, LANES)),
        tile(HEAD_DIM),
        tile(HEAD_DIM),
        resident((CONV_WIDTH, 2 * D_MLSTM)),
        resident((1, 2 * D_MLSTM)),
        resident((RET_HEADS, CHUNK, CHUNK)),
        resident((RET_HEADS, CHUNK, HEAD_DIM)),
        resident((RET_HEADS, CHUNK, HEAD_DIM)),
        resident((RET_HEADS, 1, HEAD_DIM)),
        resident((CHUNK, CHUNK)),
        resident((2 * HEAD_DIM, 2 * HEAD_DIM)),
        resident((1, D_RET)),
        resident((1, D_MLSTM)),
        resident((d_mix, d)),
    ]
    head_tiles = lambda heads, dtype: pltpu.VMEM((n_chunks, heads, HEAD_DIM, CHUNK), dtype)
    scratch_shapes = [
        pltpu.VMEM((rows, d), BF16),
        pltpu.VMEM((rows, Z_WIDTH), F32),
        pltpu.VMEM((rows + SUBLANES, 2 * D_MLSTM), F32),
        pltpu.VMEM((rows, LANES), F32),
        pltpu.VMEM((rows, d_mix), F32),
        pltpu.VMEM((rows, d_mix), BF16),
        pltpu.VMEM((rows, D_RET), BF16),
        pltpu.VMEM((rows, D_RET), BF16),
        pltpu.VMEM((rows, D_RET), BF16),
        pltpu.VMEM((rows, D_RET), BF16),
        head_tiles(RET_HEADS, BF16),
        pltpu.VMEM((rows, D_MLSTM), BF16),
        pltpu.VMEM((rows, D_MLSTM), BF16),
        pltpu.VMEM((rows, 2 * D_MLSTM), BF16),
        head_tiles(MLSTM_HEADS, F32),
        pltpu.VMEM((rows, D_MLSTM), F32),
        pltpu.VMEM((rows, D_MLSTM), F32),
        pltpu.VMEM((n_chunks, SUBLANES, CHUNK), F32),
        pltpu.VMEM((RET_HEADS, HEAD_DIM, HEAD_DIM), F32),
        pltpu.VMEM((MLSTM_HEADS, HEAD_DIM, 2 * HEAD_DIM), F32),
        pltpu.VMEM((MLSTM_HEADS, 1, LANES), F32),
    ]
    return pl.pallas_call(
        _mixer_kernel,
        grid=(b, s // rows),
        in_specs=in_specs,
        out_specs=tile(d),
        out_shape=jax.ShapeDtypeStruct(x.shape, x.dtype),
        scratch_shapes=scratch_shapes,
        compiler_params=pltpu.CompilerParams(
            dimension_semantics=("arbitrary", "arbitrary"), vmem_limit_bytes=VMEM_LIMIT_BYTES),
        name="mixer",
    )(x, mod, g.reshape(1, d), win, gbias, cos, sin, conv_w, conv_b.reshape(1, -1),
      decay, w_q, w_k, cdec, tril, avg, g_ret.reshape(1, -1), g_ml.reshape(1, -1),
      w_out.astype(BF16))


@jax.jit
def kernel(x, c, positions, norm_g, w_ada, b_ada, w_ff1, w_ff3, w_ff2, w_in, conv_w, conv_b,
           b_igate, b_fgate, g_ret_norm, g_mlstm_norm, w_out, g_final):
    depth = w_in.shape[0]
    ffn_rows, mix_rows = _tiles(x.shape[1])
    mods = _ada_mod(c, w_ada, b_ada)
    cos, sin = _rope_tables(positions, mix_rows)
    for l in range(depth):
        x = _ffn(x, mods[l, 0], norm_g[l, 0], w_ff1[l, 0], w_ff3[l, 0], w_ff2[l, 0], ffn_rows)
        x = _mixer(x, mods[l, 1], norm_g[l, 1], w_in[l], conv_w[l], conv_b[l], b_igate[l],
                   b_fgate[l], g_ret_norm[l], g_mlstm_norm[l], w_out[l], cos, sin, mix_rows)
        x = _ffn(x, mods[l, 2], norm_g[l, 2], w_ff1[l, 1], w_ff3[l, 1], w_ff2[l, 1], ffn_rows,
                 g_final=g_final if l == depth - 1 else None)
    return x
```

```python
import functools

import jax
import jax.numpy as jnp
from jax import lax
from jax.experimental import pallas as pl
from jax.experimental.pallas import tpu as pltpu

F32 = jnp.float32
BF16 = jnp.bfloat16

RET_HEADS = 4
MLSTM_HEADS = 4
HEAD_DIM = 128
CHUNK = 128
CONV_WIDTH = 4
ROPE_BASE = 10000.0
EPS = 1e-6
N_SUBLAYERS = 3
D_RET = RET_HEADS * HEAD_DIM
D_MLSTM = MLSTM_HEADS * HEAD_DIM

LANES = 128
SUBLANES = 8
MXU_COLS = 256
VMEM_LIMIT_BYTES = 56 * 1024 * 1024

COL_RQ = 0
COL_RK = D_RET
COL_RV = 2 * D_RET
COL_RG = 3 * D_RET
COL_MQK = 4 * D_RET
COL_MV = COL_MQK + 2 * D_MLSTM
COL_MO = COL_MV + D_MLSTM
COL_GATES = COL_MO + D_MLSTM
D_IN_PADDED = COL_GATES + LANES

NT_DIMS = (((1,), (1,)), ((), ()))


def _tiles(seq):
    ffn_rows = min(512, seq)
    mix_rows = min(512, seq)
    assert seq % ffn_rows == 0 and seq % mix_rows == 0 and mix_rows % CHUNK == 0
    return ffn_rows, mix_rows


def _dot(a, b):
    return jnp.dot(a, b, preferred_element_type=F32)


def _rmsnorm(x, g):
    return x * lax.rsqrt(jnp.mean(x * x, axis=-1, keepdims=True) + EPS) * g


def _modulated_norm(x, mod, g):
    shift, scale = mod[0:1], mod[1:2]
    return _rmsnorm(x, g) * (1.0 + scale) + shift


def _log_sigmoid(x):
    return jnp.minimum(x, 0.0) - jnp.log1p(jnp.exp(-jnp.abs(x)))


def _ada_kernel(c_ref, w_ref, b_ref, o_ref):
    sc = jax.nn.silu(c_ref[...])
    o_ref[...] = jnp.dot(sc, w_ref[...], preferred_element_type=F32,
                         precision=lax.Precision.HIGHEST) + b_ref[...]


def _ada_mod(c, w_ada, b_ada):
    depth, nsub, d, d3 = w_ada.shape
    n = depth * nsub
    b = c.shape[0]
    out = pl.pallas_call(
        _ada_kernel,
        grid=(n, d3 // d),
        in_specs=[
            pl.BlockSpec((b, d), lambda i, j: (0, 0)),
            pl.BlockSpec((None, d, d), lambda i, j: (i, 0, j)),
            pl.BlockSpec((None, 1, d), lambda i, j: (i, 0, j)),
        ],
        out_specs=pl.BlockSpec((None, b, d), lambda i, j: (i, 0, j)),
        out_shape=jax.ShapeDtypeStruct((n, b, d3), F32),
        compiler_params=pltpu.CompilerParams(
            dimension_semantics=("arbitrary", "arbitrary"), vmem_limit_bytes=VMEM_LIMIT_BYTES),
        name="ada_mod",
    )(c, w_ada.reshape(n, d, d3), b_ada.reshape(n, 1, d3))
    return out.reshape(depth, nsub, b, d3 // d, d)


def _rope_kernel(pos_ref, invf_ref, sign_ref, cos_ref, sin_ref):
    ang = pos_ref[...].astype(F32) * invf_ref[...]
    cos_ref[...] = jnp.cos(ang)
    sin_ref[...] = jnp.sin(ang) * sign_ref[...]


def _rope_tables(positions, rows):
    b, s = positions.shape
    half = HEAD_DIM // 2
    inv_freq = ROPE_BASE ** (-jnp.arange(0, HEAD_DIM, 2, dtype=F32) / HEAD_DIM)
    invf = jnp.concatenate([inv_freq, inv_freq]).reshape(1, HEAD_DIM)
    sign = jnp.concatenate([-jnp.ones((half,), F32), jnp.ones((half,), F32)]).reshape(1, HEAD_DIM)
    table = jax.ShapeDtypeStruct((b, s, HEAD_DIM), F32)
    const = pl.BlockSpec((1, HEAD_DIM), lambda i, j: (0, 0))
    tile = pl.BlockSpec((None, rows, HEAD_DIM), lambda i, j: (i, j, 0))
    return pl.pallas_call(
        _rope_kernel,
        grid=(b, s // rows),
        in_specs=[pl.BlockSpec((None, rows, 1), lambda i, j: (i, j, 0)), const, const],
        out_specs=[tile, tile],
        out_shape=[table, table],
        compiler_params=pltpu.CompilerParams(
            dimension_semantics=("arbitrary", "arbitrary"), vmem_limit_bytes=VMEM_LIMIT_BYTES),
        name="rope_tables",
    )(positions.reshape(b, s, 1), invf, sign)


def _ffn_kernel(x_ref, mod_ref, g_ref, w1_ref, w3_ref, w2_ref, *rest, final_norm):
    if final_norm:
        gf_ref, o_ref, act_ref = rest
    else:
        o_ref, act_ref = rest
    x = x_ref[...]
    mod = mod_ref[...]
    h = _modulated_norm(x, mod, g_ref[...]).astype(BF16)
    d_ff = w1_ref.shape[1]
    for c0 in range(0, d_ff, MXU_COLS):
        a = _dot(h, w1_ref[:, c0:c0 + MXU_COLS])
        b = _dot(h, w3_ref[:, c0:c0 + MXU_COLS])
        act_ref[:, c0:c0 + MXU_COLS] = (jax.nn.silu(a) * b).astype(BF16)
    out = x + 0.5 * mod[2:3] * _dot(act_ref[...], w2_ref[...])
    if final_norm:
        out = _rmsnorm(out, gf_ref[...])
    o_ref[...] = out


def _ffn(x, mod, g, w1, w3, w2, rows, g_final=None):
    b, s, d = x.shape
    d_ff = w1.shape[1]
    assert d_ff % MXU_COLS == 0
    final_norm = g_final is not None
    resident = lambda shape: pl.BlockSpec(shape, lambda i, j: (0, 0), pipeline_mode=pl.Buffered(1))
    tile = pl.BlockSpec((None, rows, d), lambda i, j: (i, j, 0))
    in_specs = [
        tile,
        pl.BlockSpec((None, 3, d), lambda i, j: (i, 0, 0)),
        resident((1, d)),
        resident((d, d_ff)),
        resident((d, d_ff)),
        resident((d_ff, d)),
    ]
    args = [x, mod, g.reshape(1, d), w1.astype(BF16), w3.astype(BF16), w2.astype(BF16)]
    if final_norm:
        in_specs.append(resident((1, d)))
        args.append(g_final.reshape(1, d))
    return pl.pallas_call(
        functools.partial(_ffn_kernel, final_norm=final_norm),
        grid=(b, s // rows),
        in_specs=in_specs,
        out_specs=tile,
        out_shape=jax.ShapeDtypeStruct(x.shape, x.dtype),
        scratch_shapes=[pltpu.VMEM((rows, d_ff), BF16)],
        compiler_params=pltpu.CompilerParams(
            dimension_semantics=("arbitrary", "arbitrary"), vmem_limit_bytes=VMEM_LIMIT_BYTES),
        name="ffn_final" if final_norm else "ffn",
    )(*args)


Z_RQ = 0
Z_RK = D_RET
Z_RG = 2 * D_RET
Z_MO = 3 * D_RET
Z_WIDTH = 3 * D_RET + D_MLSTM


def _row_mean_lanes(t, avg):
    hi = t.astype(BF16)
    lo = (t - hi.astype(F32)).astype(BF16)
    return _dot(hi, avg) + _dot(lo, avg)


def _prefix_max_rows(t, row_id):
    shift = 1
    while shift < CHUNK:
        t = jnp.maximum(t, jnp.where(row_id >= shift, pltpu.roll(t, shift, 0), -jnp.inf))
        shift *= 2
    return t


def _stage_retention(r, hh, z_ref, cos, sin, wq_ref, wk_ref, rq_ref, rqw_ref, rk_ref, rkwt_ref):
    r0, c0 = r * CHUNK, hh * HEAD_DIM
    q = z_ref[r0:r0 + CHUNK, Z_RQ + c0:Z_RQ + c0 + HEAD_DIM]
    k = z_ref[r0:r0 + CHUNK, Z_RK + c0:Z_RK + c0 + HEAD_DIM]
    half = HEAD_DIM // 2
    qr = q * cos + pltpu.roll(q, half, 1) * sin
    kr = (k * cos + pltpu.roll(k, half, 1) * sin) * (HEAD_DIM ** -0.5)
    rq_ref[r0:r0 + CHUNK, c0:c0 + HEAD_DIM] = qr.astype(BF16)
    rqw_ref[r0:r0 + CHUNK, c0:c0 + HEAD_DIM] = (qr * wq_ref[hh]).astype(BF16)
    rk_ref[r0:r0 + CHUNK, c0:c0 + HEAD_DIM] = kr.astype(BF16)
    rkwt_ref[r, hh] = (kr * wk_ref[hh]).T.astype(BF16)


def _stage_mlstm_qk(r, blk, qkraw_ref, convw_ref, convb_ref, mq_ref, mk_ref, mkt_ref):
    r0, c0 = r * CHUNK, blk * HEAD_DIM
    acc = convb_ref[:, c0:c0 + HEAD_DIM]
    for j in range(CONV_WIDTH):
        start = SUBLANES - (CONV_WIDTH - 1) + j + r0
        acc = acc + (convw_ref[j:j + 1, c0:c0 + HEAD_DIM]
                     * qkraw_ref[start:start + CHUNK, c0:c0 + HEAD_DIM])
    qk = jax.nn.silu(acc)
    if blk < MLSTM_HEADS:
        mq_ref[r0:r0 + CHUNK, c0:c0 + HEAD_DIM] = (qk * (HEAD_DIM ** -0.5)).astype(BF16)
    else:
        hh = blk - MLSTM_HEADS
        mk_ref[r0:r0 + CHUNK, hh * HEAD_DIM:(hh + 1) * HEAD_DIM] = qk.astype(BF16)
        mkt_ref[r, hh] = qk.T


def _stage_gates(r, gate_ref, tril_ref, row_id, cmb_ref, bb_ref, arow_ref):
    r0 = r * CHUNK
    gates = gate_ref[r0:r0 + CHUNK, :]
    bcum = jnp.dot(tril_ref[...], gates, preferred_element_type=F32,
                   precision=lax.Precision.HIGHEST)
    a = gates - pltpu.roll(bcum, LANES - MLSTM_HEADS, 1)
    cmax = _prefix_max_rows(a, row_id)
    arow_ref[r] = a.T[0:SUBLANES, :]
    for hh in range(MLSTM_HEADS):
        cols = slice(hh * HEAD_DIM, (hh + 1) * HEAD_DIM)
        cmb_ref[r0:r0 + CHUNK, cols] = jnp.broadcast_to(cmax[:, hh:hh + 1], (CHUNK, HEAD_DIM))
        fcol = MLSTM_HEADS + hh
        bb_ref[r0:r0 + CHUNK, cols] = jnp.broadcast_to(bcum[:, fcol:fcol + 1], (CHUNK, HEAD_DIM))


def _retention_chunk(hh, r, rq_ref, rqw_ref, rk_ref, rv_ref, rkwt_ref, decay_ref, cdec_ref,
                     state_ref, pre_ref):
    rows = slice(r * CHUNK, (r + 1) * CHUNK)
    cols = slice(hh * HEAD_DIM, (hh + 1) * HEAD_DIM)
    scores = lax.dot_general(rq_ref[rows, cols], rk_ref[rows, cols], NT_DIMS,
                             preferred_element_type=F32)
    yield
    vb = rv_ref[rows, cols]
    state = state_ref[hh]
    intra = _dot((scores * decay_ref[hh]).astype(BF16), vb)
    inter = _dot(rqw_ref[rows, cols], state.astype(BF16))
    update = _dot(rkwt_ref[r, hh], vb)
    yield
    pre_ref[rows, cols] = intra + inter
    state_ref[hh] = state * cdec_ref[hh] + update


def _mlstm_chunk(hh, r, causal, mq_ref, mk_ref, mv_ref, mkt_ref, cmb_ref, bb_ref, arow_ref,
                 c_ref, m_ref, pre_ref):
    rows = slice(r * CHUNK, (r + 1) * CHUNK)
    c0 = hh * HEAD_DIM
    cols = slice(c0, c0 + HEAD_DIM)
    qb = mq_ref[rows, cols]
    scores = lax.dot_general(qb, mk_ref[rows, cols], NT_DIMS, preferred_element_type=F32)
    yield
    v_aug = mv_ref[rows, 2 * c0:2 * c0 + 2 * HEAD_DIM]
    cmax = cmb_ref[rows, cols]
    b_l = bb_ref[rows, cols]
    a_row = arow_ref[r][hh:hh + 1, :]
    m_prev = m_ref[hh]
    state = c_ref[hh]
    mx = jnp.maximum(cmax, m_prev)
    w_d = jnp.exp(jnp.where(causal, a_row - mx, -jnp.inf))
    intra = _dot((scores * w_d).astype(BF16), v_aug)
    inter = _dot(qb, state.astype(BF16))
    b_last = b_l[CHUNK - 1:CHUNK, :]
    m_new = jnp.maximum(b_last + m_prev, b_last + cmax[CHUNK - 1:CHUNK, :])
    w_g = jnp.exp(b_last + a_row - m_new)
    update = _dot((mkt_ref[r, hh] * w_g).astype(BF16), v_aug)
    yield
    w_inter = jnp.exp(m_prev - mx)
    tot = intra + jnp.concatenate([w_inter, w_inter], axis=1) * inter
    num, den = tot[:, :HEAD_DIM], tot[:, HEAD_DIM:]
    h = num / jnp.maximum(jnp.abs(den), jnp.exp(-(b_l + mx)))
    pre_ref[rows, D_RET + c0:D_RET + c0 + HEAD_DIM] = h
    w_c = jnp.exp(b_last + m_prev - m_new)
    c_ref[hh] = jnp.concatenate([w_c, w_c], axis=1) * state + update
    m_ref[hh] = m_new


def _run_staggered(tasks, stages, fillers):
    n = len(tasks)
    n_steps = n + stages - 1
    for step in range(n_steps):
        for run_filler in fillers.get(step, ()):
            run_filler()
        for s in range(stages):
            j = step - s
            if 0 <= j < n:
                next(tasks[j], None)
    assert all(step < n_steps for step in fillers)


def _mixer_kernel(x_ref, mod_ref, g_ref, win_ref, gbias_ref, cos_ref, sin_ref, convw_ref, convb_ref,
                  decay_ref, wq_ref, wk_ref, cdec_ref, tril_ref, avg_ref, gret_ref, gml_ref, wout_ref,
                  o_ref,
                  h_ref, z_ref, qkraw_ref, gate_ref, pre_ref, y_ref,
                  rq_ref, rqw_ref, rk_ref, rv_ref, rkwt_ref, mq_ref, mk_ref, mv_ref, mkt_ref,
                  cmb_ref, bb_ref, arow_ref, sret_ref, c_ref, m_ref):
    rows_total = x_ref.shape[0]
    n_chunks = rows_total // CHUNK

    @pl.when(pl.program_id(1) == 0)
    def _start_of_sequence():
        sret_ref[...] = jnp.zeros_like(sret_ref)
        c_ref[...] = jnp.zeros_like(c_ref)
        m_ref[...] = jnp.zeros_like(m_ref)
        qkraw_ref[0:SUBLANES, :] = jnp.zeros((SUBLANES, qkraw_ref.shape[1]), F32)
        for hh in range(MLSTM_HEADS):
            ones_cols = slice((2 * hh + 1) * HEAD_DIM, (2 * hh + 2) * HEAD_DIM)
            mv_ref[:, ones_cols] = jnp.ones((rows_total, HEAD_DIM), BF16)

    mod = mod_ref[...]
    h_ref[...] = _modulated_norm(x_ref[...], mod, g_ref[...]).astype(BF16)

    def project(c0, width=D_RET):
        return _dot(h_ref[...], win_ref[:, c0:c0 + width])

    row_id = lax.broadcasted_iota(jnp.int32, (CHUNK, CHUNK), 0)
    col_id = lax.broadcasted_iota(jnp.int32, (CHUNK, CHUNK), 1)
    causal = row_id >= col_id

    def project_to_z(col, zcol, width=D_RET):
        z_ref[:, zcol:zcol + width] = project(col, width)

    def project_qk(c0):
        qkraw_ref[SUBLANES:SUBLANES + rows_total, c0:c0 + D_RET] = project(COL_MQK + c0)

    def project_gates():
        zg = project(COL_GATES, LANES) + gbias_ref[...]
        lane = lax.broadcasted_iota(jnp.int32, zg.shape, 1)
        gate_ref[...] = jnp.where(lane < MLSTM_HEADS, zg, _log_sigmoid(zg))

    def project_rv():
        rv_ref[...] = project(COL_RV).astype(BF16)

    def project_mv():
        mv = project(COL_MV)
        for hh in range(MLSTM_HEADS):
            mv_ref[:, 2 * hh * HEAD_DIM:(2 * hh + 1) * HEAD_DIM] = (
                mv[:, hh * HEAD_DIM:(hh + 1) * HEAD_DIM].astype(BF16))

    def stage_retention(r):
        cos = cos_ref[r * CHUNK:(r + 1) * CHUNK, :]
        sin = sin_ref[r * CHUNK:(r + 1) * CHUNK, :]
        for hh in range(RET_HEADS):
            _stage_retention(r, hh, z_ref, cos, sin, wq_ref, wk_ref, rq_ref, rqw_ref, rk_ref,
                             rkwt_ref)

    def stage_mlstm(r):
        for blk in range(2 * MLSTM_HEADS):
            _stage_mlstm_qk(r, blk, qkraw_ref, convw_ref, convb_ref, mq_ref, mk_ref, mkt_ref)
        _stage_gates(r, gate_ref, tril_ref, row_id, cmb_ref, bb_ref, arow_ref)

    project_qk(0)
    project_qk(D_RET)
    project_gates()
    projections = [
        functools.partial(project_to_z, COL_RQ, Z_RQ),
        functools.partial(project_to_z, COL_RK, Z_RK),
        project_rv,
        project_mv,
    ]
    for r in range(n_chunks):
        stage_mlstm(r)
        if projections:
            projections.pop(0)()
    for run_projection in projections:
        run_projection()
    qkraw_ref[0:SUBLANES, :] = qkraw_ref[rows_total:rows_total + SUBLANES, :]
    stage_retention(0)

    tasks = []
    for r in range(n_chunks):
        for hh in range(max(RET_HEADS, MLSTM_HEADS)):
            if hh < RET_HEADS:
                tasks.append(_retention_chunk(hh, r, rq_ref, rqw_ref, rk_ref, rv_ref, rkwt_ref,
                                              decay_ref, cdec_ref, sret_ref, pre_ref))
            if hh < MLSTM_HEADS:
                tasks.append(_mlstm_chunk(hh, r, causal, mq_ref, mk_ref, mv_ref, mkt_ref,
                                          cmb_ref, bb_ref, arow_ref, c_ref, m_ref, pre_ref))
    n_stages = 3
    n_steps = len(tasks) + n_stages - 1
    tasks_per_chunk = len(tasks) // n_chunks
    fillers = {}
    for r in range(1, n_chunks):
        fillers.setdefault((r - 1) * tasks_per_chunk, []).append(
            functools.partial(stage_retention, r))
    late = [(col + c0, zcol + c0) for col, zcol in ((COL_RG, Z_RG), (COL_MO, Z_MO))
            for c0 in range(0, D_RET, MXU_COLS)]
    for k, (col, zcol) in enumerate(late):
        fillers.setdefault(1 + k * (n_steps // len(late)), []).append(
            functools.partial(project_to_z, col, zcol, MXU_COLS))
    _run_staggered(tasks, n_stages, fillers)

    pair = 2 * HEAD_DIM
    avg = avg_ref[...]
    for p0 in range(0, D_RET + D_MLSTM, pair):
        pre = pre_ref[:, p0:p0 + pair]
        if p0 >= D_RET:
            m0 = Z_MO + p0 - D_RET
            pre = jax.nn.sigmoid(z_ref[:, m0:m0 + pair]) * pre
        d = pre - _row_mean_lanes(pre, avg)
        var = _row_mean_lanes(d * d, avg)
        if p0 < D_RET:
            hn = (jax.nn.silu(z_ref[:, Z_RG + p0:Z_RG + p0 + pair])
                  * (d * lax.rsqrt(var + EPS) * gret_ref[:, p0:p0 + pair]))
        else:
            hn = d * lax.rsqrt(var + EPS) * gml_ref[:, p0 - D_RET:p0 - D_RET + pair]
        y_ref[:, p0:p0 + pair] = hn.astype(y_ref.dtype)
    o_ref[...] = x_ref[...] + mod[2:3] * _dot(y_ref[...], wout_ref[...])


def _retention_constants():
    heads = jnp.arange(RET_HEADS, dtype=F32)
    log_gamma = jnp.log(1.0 - 2.0 ** (-5.0 - heads))
    idx = jnp.arange(CHUNK)
    diff = (idx[:, None] - idx[None, :]).astype(F32)
    decay = jnp.where(diff >= 0, jnp.exp(log_gamma[:, None, None] * jnp.maximum(diff, 0.0)), 0.0)
    w_k = jnp.exp(log_gamma[:, None] * (CHUNK - 1 - idx).astype(F32))
    w_q = jnp.exp(log_gamma[:, None] * (idx + 1).astype(F32))
    chunk_decay = jnp.exp(log_gamma * CHUNK)
    bcast = lambda t: jnp.broadcast_to(t[:, :, None], (RET_HEADS, CHUNK, HEAD_DIM))
    cdec = jnp.broadcast_to(chunk_decay[:, None, None], (RET_HEADS, 1, HEAD_DIM))
    return decay, bcast(w_q), bcast(w_k), cdec


def _mixer(x, mod, g, w_in, conv_w, conv_b, b_igate, b_fgate, g_ret, g_ml, w_out, cos, sin, rows):
    b, s, d = x.shape
    d_mix = D_RET + D_MLSTM
    n_gates = 2 * MLSTM_HEADS
    n_chunks = rows // CHUNK
    assert w_in.shape == (d, COL_GATES + n_gates) and w_out.shape == (d_mix, d)
    win = jnp.pad(w_in, ((0, 0), (0, D_IN_PADDED - w_in.shape[1]))).astype(BF16)
    gbias = jnp.pad(jnp.concatenate([b_igate, b_fgate]), (0, LANES - n_gates)).reshape(1, LANES)
    decay, w_q, w_k, cdec = _retention_constants()
    tril = jnp.tril(jnp.ones((CHUNK, CHUNK), F32))
    head_of = jnp.arange(2 * HEAD_DIM) // HEAD_DIM
    avg = jnp.where(head_of[:, None] == head_of[None, :], 1.0 / HEAD_DIM, 0.0).astype(BF16)

    def resident(shape):
        zeros = (0,) * len(shape)
        return pl.BlockSpec(shape, lambda i, j: zeros, pipeline_mode=pl.Buffered(1))

    tile = lambda width: pl.BlockSpec((None, rows, width), lambda i, j: (i, j, 0))
    in_specs = [
        tile(d),
        pl.BlockSpec((None, 3, d), lambda i, j: (i, 0, 0)),
        resident((1, d)),
        resident((d, D_IN_PADDED)),
        resident((1, LANES)),
        tile(HEAD_DIM),
        tile(HEAD_DIM),
        resident((CONV_WIDTH, 2 * D_MLSTM)),
        resident((1, 2 * D_MLSTM)),
        resident((RET_HEADS, CHUNK, CHUNK)),
        resident((RET_HEADS, CHUNK, HEAD_DIM)),
        resident((RET_HEADS, CHUNK, HEAD_DIM)),
        resident((RET_HEADS, 1, HEAD_DIM)),
        resident((CHUNK, CHUNK)),
        resident((2 * HEAD_DIM, 2 * HEAD_DIM)),
        resident((1, D_RET)),
        resident((1, D_MLSTM)),
        resident((d_mix, d)),
    ]
    head_tiles = lambda heads, dtype: pltpu.VMEM((n_chunks, heads, HEAD_DIM, CHUNK), dtype)
    scratch_shapes = [
        pltpu.VMEM((rows, d), BF16),
        pltpu.VMEM((rows, Z_WIDTH), F32),
        pltpu.VMEM((rows + SUBLANES, 2 * D_MLSTM), F32),
        pltpu.VMEM((rows, LANES), F32),
        pltpu.VMEM((rows, d_mix), F32),
        pltpu.VMEM((rows, d_mix), BF16),
        pltpu.VMEM((rows, D_RET), BF16),
        pltpu.VMEM((rows, D_RET), BF16),
        pltpu.VMEM((rows, D_RET), BF16),
        pltpu.VMEM((rows, D_RET), BF16),
        head_tiles(RET_HEADS, BF16),
        pltpu.VMEM((rows, D_MLSTM), BF16),
        pltpu.VMEM((rows, D_MLSTM), BF16),
        pltpu.VMEM((rows, 2 * D_MLSTM), BF16),
        head_tiles(MLSTM_HEADS, F32),
        pltpu.VMEM((rows, D_MLSTM), F32),
        pltpu.VMEM((rows, D_MLSTM), F32),
        pltpu.VMEM((n_chunks, SUBLANES, CHUNK), F32),
        pltpu.VMEM((RET_HEADS, HEAD_DIM, HEAD_DIM), F32),
        pltpu.VMEM((MLSTM_HEADS, HEAD_DIM, 2 * HEAD_DIM), F32),
        pltpu.VMEM((MLSTM_HEADS, 1, LANES), F32),
    ]
    return pl.pallas_call(
        _mixer_kernel,
        grid=(b, s // rows),
        in_specs=in_specs,
        out_specs=tile(d),
        out_shape=jax.ShapeDtypeStruct(x.shape, x.dtype),
        scratch_shapes=scratch_shapes,
        compiler_params=pltpu.CompilerParams(
            dimension_semantics=("arbitrary", "arbitrary"), vmem_limit_bytes=VMEM_LIMIT_BYTES),
        name="mixer",
    )(x, mod, g.reshape(1, d), win, gbias, cos, sin, conv_w, conv_b.reshape(1, -1),
      decay, w_q, w_k, cdec, tril, avg, g_ret.reshape(1, -1), g_ml.reshape(1, -1),
      w_out.astype(BF16))


@jax.jit
def kernel(x, c, positions, norm_g, w_ada, b_ada, w_ff1, w_ff3, w_ff2, w_in, conv_w, conv_b,
           b_igate, b_fgate, g_ret_norm, g_mlstm_norm, w_out, g_final):
    depth = w_in.shape[0]
    ffn_rows, mix_rows = _tiles(x.shape[1])
    mods = _ada_mod(c, w_ada, b_ada)
    cos, sin = _rope_tables(positions, mix_rows)
    for l in range(depth):
        x = _ffn(x, mods[l, 0], norm_g[l, 0], w_ff1[l, 0], w_ff3[l, 0], w_ff2[l, 0], ffn_rows)
        x = _mixer(x, mods[l, 1], norm_g[l, 1], w_in[l], conv_w[l], conv_b[l], b_igate[l],
                   b_fgate[l], g_ret_norm[l], g_mlstm_norm[l], w_out[l], cos, sin, mix_rows)
        x = _ffn(x, mods[l, 2], norm_g[l, 2], w_ff1[l, 1], w_ff3[l, 1], w_ff2[l, 1], ffn_rows,
                 g_final=g_final if l == depth - 1 else None)
    return x
```

```python
import functools

import jax
import jax.numpy as jnp
from jax import lax
from jax.experimental import pallas as pl
from jax.experimental.pallas import tpu as pltpu

F32 = jnp.float32
BF16 = jnp.bfloat16

RET_HEADS = 4
MLSTM_HEADS = 4
HEAD_DIM = 128
CHUNK = 128
CONV_WIDTH = 4
ROPE_BASE = 10000.0
EPS = 1e-6
N_SUBLAYERS = 3
D_RET = RET_HEADS * HEAD_DIM
D_MLSTM = MLSTM_HEADS * HEAD_DIM

LANES = 128
SUBLANES = 8
MXU_COLS = 256
VMEM_LIMIT_BYTES = 56 * 1024 * 1024

COL_RQ = 0
COL_RK = D_RET
COL_RV = 2 * D_RET
COL_RG = 3 * D_RET
COL_MQK = 4 * D_RET
COL_MV = COL_MQK + 2 * D_MLSTM
COL_MO = COL_MV + D_MLSTM
COL_GATES = COL_MO + D_MLSTM

NT_DIMS = (((1,), (1,)), ((), ()))


def _tiles(seq):
    ffn_rows = min(512, seq)
    mix_rows = min(512, seq)
    assert seq % ffn_rows == 0 and seq % mix_rows == 0 and mix_rows % CHUNK == 0
    return ffn_rows, mix_rows


def _dot(a, b):
    return jnp.dot(a, b, preferred_element_type=F32)


def _rmsnorm(x, g):
    return x * lax.rsqrt(jnp.mean(x * x, axis=-1, keepdims=True) + EPS) * g


def _modulated_norm(x, mod, g):
    shift, scale = mod[0:1], mod[1:2]
    return _rmsnorm(x, g) * (1.0 + scale) + shift


def _log_sigmoid(x):
    return jnp.minimum(x, 0.0) - jnp.log1p(jnp.exp(-jnp.abs(x)))


def _ada_kernel(c_ref, w_ref, b_ref, o_ref):
    sc = jax.nn.silu(c_ref[...])
    o_ref[...] = jnp.dot(sc, w_ref[...], preferred_element_type=F32,
                         precision=lax.Precision.HIGHEST) + b_ref[...]


def _ada_mod(c, w_ada, b_ada):
    depth, nsub, d, d3 = w_ada.shape
    n = depth * nsub
    b = c.shape[0]
    out = pl.pallas_call(
        _ada_kernel,
        grid=(n, d3 // d),
        in_specs=[
            pl.BlockSpec((b, d), lambda i, j: (0, 0)),
            pl.BlockSpec((None, d, d), lambda i, j: (i, 0, j)),
            pl.BlockSpec((None, 1, d), lambda i, j: (i, 0, j)),
        ],
        out_specs=pl.BlockSpec((None, b, d), lambda i, j: (i, 0, j)),
        out_shape=jax.ShapeDtypeStruct((n, b, d3), F32),
        compiler_params=pltpu.CompilerParams(
            dimension_semantics=("arbitrary", "arbitrary"), vmem_limit_bytes=VMEM_LIMIT_BYTES),
        name="ada_mod",
    )(c, w_ada.reshape(n, d, d3), b_ada.reshape(n, 1, d3))
    return out.reshape(depth, nsub, b, d3 // d, d)


def _rope_kernel(pos_ref, invf_ref, cos_ref, sin_ref):
    rows, half = pos_ref.shape[0], HEAD_DIM // 2
    pos = pos_ref[...].astype(F32)
    low = lax.broadcasted_iota(jnp.int32, (rows // 2, HEAD_DIM), 1) < half
    ang = jnp.where(low, pos[:rows // 2], pos[rows // 2:]) * invf_ref[...]
    c, s = jnp.cos(ang), jnp.sin(ang)
    c_swapped, s_swapped = pltpu.roll(c, half, 1), pltpu.roll(s, half, 1)
    cos_ref[:rows // 2, :] = jnp.where(low, c, c_swapped)
    cos_ref[rows // 2:, :] = jnp.where(low, c_swapped, c)
    sin_ref[:rows // 2, :] = jnp.where(low, -s, s_swapped)
    sin_ref[rows // 2:, :] = jnp.where(low, -s_swapped, s)


def _rope_tables(positions, rows):
    b, s = positions.shape
    assert rows % (2 * SUBLANES) == 0
    inv_freq = ROPE_BASE ** (-jnp.arange(0, HEAD_DIM, 2, dtype=F32) / HEAD_DIM)
    invf = jnp.concatenate([inv_freq, inv_freq]).reshape(1, HEAD_DIM)
    table = jax.ShapeDtypeStruct((b, s, HEAD_DIM), F32)
    const = pl.BlockSpec((1, HEAD_DIM), lambda i, j: (0, 0))
    tile = pl.BlockSpec((None, rows, HEAD_DIM), lambda i, j: (i, j, 0))
    return pl.pallas_call(
        _rope_kernel,
        grid=(b, s // rows),
        in_specs=[pl.BlockSpec((None, rows, 1), lambda i, j: (i, j, 0)), const],
        out_specs=[tile, tile],
        out_shape=[table, table],
        compiler_params=pltpu.CompilerParams(
            dimension_semantics=("arbitrary", "arbitrary"), vmem_limit_bytes=VMEM_LIMIT_BYTES),
        name="rope_tables",
    )(positions.reshape(b, s, 1), invf)


def _ffn_kernel(x_ref, mod_ref, g_ref, w1_ref, w3_ref, w2_ref, *rest, final_norm):
    if final_norm:
        gf_ref, o_ref, act_ref = rest
    else:
        o_ref, act_ref = rest
    x = x_ref[...]
    mod = mod_ref[...]
    h = _modulated_norm(x, mod, g_ref[...]).astype(BF16)
    d_ff = w1_ref.shape[1]
    for c0 in range(0, d_ff, MXU_COLS):
        a = _dot(h, w1_ref[:, c0:c0 + MXU_COLS])
        b = _dot(h, w3_ref[:, c0:c0 + MXU_COLS])
        act_ref[:, c0:c0 + MXU_COLS] = (jax.nn.silu(a) * b).astype(BF16)
    out = x + 0.5 * mod[2:3] * _dot(act_ref[...], w2_ref[...])
    if final_norm:
        out = _rmsnorm(out, gf_ref[...])
    o_ref[...] = out


def _ffn(x, mod, g, which, w1, w3, w2, rows, g_final=None):
    b, s, d = x.shape
    d_ff = w1.shape[-1]
    assert d_ff % MXU_COLS == 0
    final_norm = g_final is not None
    resident = lambda shape: pl.BlockSpec(shape, lambda i, j: (0, 0), pipeline_mode=pl.Buffered(1))
    weight = lambda rows_, cols_: pl.BlockSpec((None, None, rows_, cols_),
                                               lambda i, j: (*which, 0, 0),
                                               pipeline_mode=pl.Buffered(1))
    tile = pl.BlockSpec((None, rows, d), lambda i, j: (i, j, 0))
    in_specs = [
        tile,
        pl.BlockSpec((None, 3, d), lambda i, j: (i, 0, 0)),
        resident((1, d)),
        weight(d, d_ff),
        weight(d, d_ff),
        weight(d_ff, d),
    ]
    args = [x, mod, g.reshape(1, d), w1, w3, w2]
    if final_norm:
        in_specs.append(resident((1, d)))
        args.append(g_final.reshape(1, d))
    return pl.pallas_call(
        functools.partial(_ffn_kernel, final_norm=final_norm),
        grid=(b, s // rows),
        in_specs=in_specs,
        out_specs=tile,
        out_shape=jax.ShapeDtypeStruct(x.shape, x.dtype),
        scratch_shapes=[pltpu.VMEM((rows, d_ff), BF16)],
        compiler_params=pltpu.CompilerParams(
            dimension_semantics=("arbitrary", "arbitrary"), vmem_limit_bytes=VMEM_LIMIT_BYTES),
        name="ffn_final" if final_norm else "ffn",
    )(*args)


Z_RQ = 0
Z_RK = D_RET
Z_RG = 2 * D_RET
Z_MO = 3 * D_RET
Z_WIDTH = 3 * D_RET + D_MLSTM


def _row_mean_lanes(t, avg):
    hi = t.astype(BF16)
    lo = (t - hi.astype(F32)).astype(BF16)
    return _dot(hi, avg) + _dot(lo, avg)


def _prefix_max_rows(t, row_id):
    shift = 1
    while shift < CHUNK:
        t = jnp.maximum(t, jnp.where(row_id >= shift, pltpu.roll(t, shift, 0), -jnp.inf))
        shift *= 2
    return t


def _stage_retention(r, hh, z_ref, cos, sin, wq_ref, wk_ref, rq_ref, rqw_ref, rk_ref, rkwt_ref):
    r0, c0 = r * CHUNK, hh * HEAD_DIM
    q = z_ref[r0:r0 + CHUNK, Z_RQ + c0:Z_RQ + c0 + HEAD_DIM]
    k = z_ref[r0:r0 + CHUNK, Z_RK + c0:Z_RK + c0 + HEAD_DIM]
    half = HEAD_DIM // 2
    qr = q * cos + pltpu.roll(q, half, 1) * sin
    kr = (k * cos + pltpu.roll(k, half, 1) * sin) * (HEAD_DIM ** -0.5)
    rq_ref[r0:r0 + CHUNK, c0:c0 + HEAD_DIM] = qr.astype(BF16)
    rqw_ref[r0:r0 + CHUNK, c0:c0 + HEAD_DIM] = (qr * wq_ref[hh]).astype(BF16)
    rk_ref[r0:r0 + CHUNK, c0:c0 + HEAD_DIM] = kr.astype(BF16)
    rkwt_ref[r, hh] = (kr * wk_ref[hh]).T.astype(BF16)


def _stage_mlstm_qk(r, blk, qkraw_ref, convw_ref, convb_ref, mq_ref, mk_ref, mkt_ref):
    r0, c0 = r * CHUNK, blk * HEAD_DIM
    acc = convb_ref[:, c0:c0 + HEAD_DIM]
    for j in range(CONV_WIDTH):
        start = SUBLANES - (CONV_WIDTH - 1) + j + r0
        acc = acc + (convw_ref[j:j + 1, c0:c0 + HEAD_DIM]
                     * qkraw_ref[start:start + CHUNK, c0:c0 + HEAD_DIM])
    qk = jax.nn.silu(acc)
    if blk < MLSTM_HEADS:
        mq_ref[r0:r0 + CHUNK, c0:c0 + HEAD_DIM] = (qk * (HEAD_DIM ** -0.5)).astype(BF16)
    else:
        hh = blk - MLSTM_HEADS
        mk_ref[r0:r0 + CHUNK, hh * HEAD_DIM:(hh + 1) * HEAD_DIM] = qk.astype(BF16)
        mkt_ref[r, hh] = qk.T


def _stage_gates(r, gate_ref, tril_ref, row_id, cmb_ref, bb_ref, arow_ref):
    r0 = r * CHUNK
    gates = gate_ref[r0:r0 + CHUNK, :]
    bcum = jnp.dot(tril_ref[...], gates, preferred_element_type=F32,
                   precision=lax.Precision.HIGHEST)
    a = gates - pltpu.roll(bcum, LANES - MLSTM_HEADS, 1)
    cmax = _prefix_max_rows(a, row_id)
    arow_ref[r] = a.T[0:SUBLANES, :]
    for hh in range(MLSTM_HEADS):
        cols = slice(hh * HEAD_DIM, (hh + 1) * HEAD_DIM)
        cmb_ref[r0:r0 + CHUNK, cols] = jnp.broadcast_to(cmax[:, hh:hh + 1], (CHUNK, HEAD_DIM))
        fcol = MLSTM_HEADS + hh
        bb_ref[r0:r0 + CHUNK, cols] = jnp.broadcast_to(bcum[:, fcol:fcol + 1], (CHUNK, HEAD_DIM))


def _retention_chunk(hh, r, rq_ref, rqw_ref, rk_ref, rv_ref, rkwt_ref, decay_ref, cdec_ref,
                     state_ref, pre_ref):
    rows = slice(r * CHUNK, (r + 1) * CHUNK)
    cols = slice(hh * HEAD_DIM, (hh + 1) * HEAD_DIM)
    scores = lax.dot_general(rq_ref[rows, cols], rk_ref[rows, cols], NT_DIMS,
                             preferred_element_type=F32)
    yield
    vb = rv_ref[rows, cols]
    state = state_ref[hh]
    intra = _dot((scores * decay_ref[hh]).astype(BF16), vb)
    inter = _dot(rqw_ref[rows, cols], state.astype(BF16))
    update = _dot(rkwt_ref[r, hh], vb)
    yield
    pre_ref[rows, cols] = intra + inter
    state_ref[hh] = state * cdec_ref[hh] + update


def _mlstm_chunk(hh, r, causal, mq_ref, mk_ref, mv_ref, mkt_ref, cmb_ref, bb_ref, arow_ref,
                 c_ref, m_ref, pre_ref):
    rows = slice(r * CHUNK, (r + 1) * CHUNK)
    c0 = hh * HEAD_DIM
    cols = slice(c0, c0 + HEAD_DIM)
    qb = mq_ref[rows, cols]
    scores = lax.dot_general(qb, mk_ref[rows, cols], NT_DIMS, preferred_element_type=F32)
    yield
    v_aug = mv_ref[rows, 2 * c0:2 * c0 + 2 * HEAD_DIM]
    cmax = cmb_ref[rows, cols]
    b_l = bb_ref[rows, cols]
    a_row = arow_ref[r][hh:hh + 1, :]
    m_prev = m_ref[hh]
    state = c_ref[hh]
    mx = jnp.maximum(cmax, m_prev)
    w_d = jnp.exp(jnp.where(causal, a_row - mx, -jnp.inf))
    intra = _dot((scores * w_d).astype(BF16), v_aug)
    inter = _dot(qb, state.astype(BF16))
    b_last = b_l[CHUNK - 1:CHUNK, :]
    m_new = jnp.maximum(b_last + m_prev, b_last + cmax[CHUNK - 1:CHUNK, :])
    w_g = jnp.exp(b_last + a_row - m_new)
    update = _dot((mkt_ref[r, hh] * w_g).astype(BF16), v_aug)
    yield
    w_inter = jnp.exp(m_prev - mx)
    tot = intra + jnp.concatenate([w_inter, w_inter], axis=1) * inter
    num, den = tot[:, :HEAD_DIM], tot[:, HEAD_DIM:]
    h = num / jnp.maximum(jnp.abs(den), jnp.exp(-(b_l + mx)))
    pre_ref[rows, D_RET + c0:D_RET + c0 + HEAD_DIM] = h
    w_c = jnp.exp(b_last + m_prev - m_new)
    c_ref[hh] = jnp.concatenate([w_c, w_c], axis=1) * state + update
    m_ref[hh] = m_new


def _run_staggered(tasks, stages, fillers):
    n = len(tasks)
    n_steps = n + stages - 1
    for step in range(n_steps):
        for run_filler in fillers.get(step, ()):
            run_filler()
        for s in range(stages):
            j = step - s
            if 0 <= j < n:
                next(tasks[j], None)
    assert all(step < n_steps for step in fillers)


def _mixer_kernel(x_ref, mod_ref, g_ref, win_ref, wgate_ref, gbias_ref, cos_ref, sin_ref, convw_ref,
                  convb_ref,
                  decay_ref, wq_ref, wk_ref, cdec_ref, tril_ref, avg_ref, gret_ref, gml_ref, wout_ref,
                  o_ref,
                  h_ref, z_ref, qkraw_ref, gate_ref, pre_ref, y_ref,
                  rq_ref, rqw_ref, rk_ref, rv_ref, rkwt_ref, mq_ref, mk_ref, mv_ref, mkt_ref,
                  cmb_ref, bb_ref, arow_ref, sret_ref, c_ref, m_ref):
    rows_total = x_ref.shape[0]
    n_chunks = rows_total // CHUNK

    @pl.when(pl.program_id(1) == 0)
    def _start_of_sequence():
        sret_ref[...] = jnp.zeros_like(sret_ref)
        c_ref[...] = jnp.zeros_like(c_ref)
        m_ref[...] = jnp.zeros_like(m_ref)
        qkraw_ref[0:SUBLANES, :] = jnp.zeros((SUBLANES, qkraw_ref.shape[1]), F32)
        for hh in range(MLSTM_HEADS):
            ones_cols = slice((2 * hh + 1) * HEAD_DIM, (2 * hh + 2) * HEAD_DIM)
            mv_ref[:, ones_cols] = jnp.ones((rows_total, HEAD_DIM), BF16)

    mod = mod_ref[...]
    h_ref[...] = _modulated_norm(x_ref[...], mod, g_ref[...]).astype(BF16)

    def project(c0, width=D_RET):
        return _dot(h_ref[...], win_ref[:, c0:c0 + width])

    row_id = lax.broadcasted_iota(jnp.int32, (CHUNK, CHUNK), 0)
    col_id = lax.broadcasted_iota(jnp.int32, (CHUNK, CHUNK), 1)
    causal = row_id >= col_id

    def project_to_z(col, zcol, width=D_RET):
        z_ref[:, zcol:zcol + width] = project(col, width)

    def project_qk(c0):
        qkraw_ref[SUBLANES:SUBLANES + rows_total, c0:c0 + D_RET] = project(COL_MQK + c0)

    def project_gates():
        zg = _dot(h_ref[...], wgate_ref[...]) + gbias_ref[...]
        lane = lax.broadcasted_iota(jnp.int32, zg.shape, 1)
        gate_ref[...] = jnp.where(lane < MLSTM_HEADS, zg, _log_sigmoid(zg))

    def project_rv():
        rv_ref[...] = project(COL_RV).astype(BF16)

    def project_mv():
        mv = project(COL_MV)
        for hh in range(MLSTM_HEADS):
            mv_ref[:, 2 * hh * HEAD_DIM:(2 * hh + 1) * HEAD_DIM] = (
                mv[:, hh * HEAD_DIM:(hh + 1) * HEAD_DIM].astype(BF16))

    def stage_retention(r):
        cos = cos_ref[r * CHUNK:(r + 1) * CHUNK, :]
        sin = sin_ref[r * CHUNK:(r + 1) * CHUNK, :]
        for hh in range(RET_HEADS):
            _stage_retention(r, hh, z_ref, cos, sin, wq_ref, wk_ref, rq_ref, rqw_ref, rk_ref,
                             rkwt_ref)

    def stage_mlstm_q(r):
        for blk in range(MLSTM_HEADS):
            _stage_mlstm_qk(r, blk, qkraw_ref, convw_ref, convb_ref, mq_ref, mk_ref, mkt_ref)

    def stage_mlstm_k(r):
        for blk in range(MLSTM_HEADS, 2 * MLSTM_HEADS):
            _stage_mlstm_qk(r, blk, qkraw_ref, convw_ref, convb_ref, mq_ref, mk_ref, mkt_ref)

    def stage_gates(r):
        _stage_gates(r, gate_ref, tril_ref, row_id, cmb_ref, bb_ref, arow_ref)

    chunks = range(n_chunks)
    half = (n_chunks + 1) // 2
    emission = [
        ([functools.partial(project_qk, 0)], []),
        ([functools.partial(project_qk, D_RET)], [(stage_mlstm_q, r) for r in chunks[:half]]),
        ([project_gates, functools.partial(project_to_z, COL_RQ, Z_RQ)],
         [(stage_mlstm_q, r) for r in chunks[half:]]),
        ([functools.partial(project_to_z, COL_RK, Z_RK)],
         [(stage_mlstm_k, r) for r in chunks[:half]]),
        ([project_rv], [(stage_mlstm_k, r) for r in chunks[half:]] + [(stage_gates, r) for r in chunks]),
        ([project_mv], [(stage_retention, 0)]),
    ]
    for run_projections, run_stages in emission:
        for run_projection in run_projections:
            run_projection()
        for stage, r in run_stages:
            stage(r)
    qkraw_ref[0:SUBLANES, :] = qkraw_ref[rows_total:rows_total + SUBLANES, :]

    tasks = []
    for r in range(n_chunks):
        for hh in range(max(RET_HEADS, MLSTM_HEADS)):
            if hh < RET_HEADS:
                tasks.append(_retention_chunk(hh, r, rq_ref, rqw_ref, rk_ref, rv_ref, rkwt_ref,
                                              decay_ref, cdec_ref, sret_ref, pre_ref))
            if hh < MLSTM_HEADS:
                tasks.append(_mlstm_chunk(hh, r, causal, mq_ref, mk_ref, mv_ref, mkt_ref,
                                          cmb_ref, bb_ref, arow_ref, c_ref, m_ref, pre_ref))
    n_stages = 3
    n_steps = len(tasks) + n_stages - 1
    tasks_per_chunk = len(tasks) // n_chunks
    fillers = {}
    for r in range(1, n_chunks):
        fillers.setdefault((r - 1) * tasks_per_chunk, []).append(
            functools.partial(stage_retention, r))
    late = [(col + c0, zcol + c0) for col, zcol in ((COL_RG, Z_RG), (COL_MO, Z_MO))
            for c0 in range(0, D_RET, MXU_COLS)]
    for k, (col, zcol) in enumerate(late):
        fillers.setdefault(1 + k * (n_steps // len(late)), []).append(
            functools.partial(project_to_z, col, zcol, MXU_COLS))
    _run_staggered(tasks, n_stages, fillers)

    pair = 2 * HEAD_DIM
    avg = avg_ref[...]
    for p0 in range(0, D_RET + D_MLSTM, pair):
        pre = pre_ref[:, p0:p0 + pair]
        if p0 >= D_RET:
            m0 = Z_MO + p0 - D_RET
            pre = jax.nn.sigmoid(z_ref[:, m0:m0 + pair]) * pre
        d = pre - _row_mean_lanes(pre, avg)
        var = _row_mean_lanes(d * d, avg)
        if p0 < D_RET:
            hn = (jax.nn.silu(z_ref[:, Z_RG + p0:Z_RG + p0 + pair])
                  * (d * lax.rsqrt(var + EPS) * gret_ref[:, p0:p0 + pair]))
        else:
            hn = d * lax.rsqrt(var + EPS) * gml_ref[:, p0 - D_RET:p0 - D_RET + pair]
        y_ref[:, p0:p0 + pair] = hn.astype(y_ref.dtype)
    o_ref[...] = x_ref[...] + mod[2:3] * _dot(y_ref[...], wout_ref[...])


def _retention_constants():
    heads = jnp.arange(RET_HEADS, dtype=F32)
    log_gamma = jnp.log(1.0 - 2.0 ** (-5.0 - heads))
    idx = jnp.arange(CHUNK)
    diff = (idx[:, None] - idx[None, :]).astype(F32)
    decay = jnp.where(diff >= 0, jnp.exp(log_gamma[:, None, None] * jnp.maximum(diff, 0.0)), 0.0)
    w_k = jnp.exp(log_gamma[:, None] * (CHUNK - 1 - idx).astype(F32))
    w_q = jnp.exp(log_gamma[:, None] * (idx + 1).astype(F32))
    chunk_decay = jnp.exp(log_gamma * CHUNK)
    bcast = lambda t: jnp.broadcast_to(t[:, :, None], (RET_HEADS, CHUNK, HEAD_DIM))
    cdec = jnp.broadcast_to(chunk_decay[:, None, None], (RET_HEADS, 1, HEAD_DIM))
    return decay, bcast(w_q), bcast(w_k), cdec


def _mixer_weights(w_in, w_out):
    n_gates = w_in.shape[-1] - COL_GATES
    w_main = w_in[:, :, :COL_GATES].astype(BF16)
    w_gate = jnp.pad(w_in[:, :, COL_GATES:], ((0, 0), (0, 0), (0, LANES - n_gates))).astype(BF16)
    return w_main, w_gate, w_out.astype(BF16)


def _mixer(x, mod, g, layer, w_main, w_gate, conv_w, conv_b, b_igate, b_fgate, g_ret, g_ml, w_out,
           cos, sin, rows):
    b, s, d = x.shape
    d_mix = D_RET + D_MLSTM
    n_gates = 2 * MLSTM_HEADS
    n_chunks = rows // CHUNK
    assert w_main.shape[1:] == (d, COL_GATES) and w_out.shape[1:] == (d_mix, d)
    gbias = jnp.pad(jnp.concatenate([b_igate, b_fgate]), (0, LANES - n_gates)).reshape(1, LANES)
    decay, w_q, w_k, cdec = _retention_constants()
    tril = jnp.tril(jnp.ones((CHUNK, CHUNK), F32))
    head_of = jnp.arange(2 * HEAD_DIM) // HEAD_DIM
    avg = jnp.where(head_of[:, None] == head_of[None, :], 1.0 / HEAD_DIM, 0.0).astype(BF16)

    def resident(shape):
        zeros = (0,) * len(shape)
        return pl.BlockSpec(shape, lambda i, j: zeros, pipeline_mode=pl.Buffered(1))

    def layer_weight(rows_, cols_):
        return pl.BlockSpec((None, rows_, cols_), lambda i, j: (layer, 0, 0),
                            pipeline_mode=pl.Buffered(1))

    tile = lambda width: pl.BlockSpec((None, rows, width), lambda i, j: (i, j, 0))
    in_specs = [
        tile(d),
        pl.BlockSpec((None, 3, d), lambda i, j: (i, 0, 0)),
        resident((1, d)),
        layer_weight(d, COL_GATES),
        layer_weight(d, LANES),
        resident((1, LANES)),
        tile(HEAD_DIM),
        tile(HEAD_DIM),
        resident((CONV_WIDTH, 2 * D_MLSTM)),
        resident((1, 2 * D_MLSTM)),
        resident((RET_HEADS, CHUNK, CHUNK)),
        resident((RET_HEADS, CHUNK, HEAD_DIM)),
        resident((RET_HEADS, CHUNK, HEAD_DIM)),
        resident((RET_HEADS, 1, HEAD_DIM)),
        resident((CHUNK, CHUNK)),
        resident((2 * HEAD_DIM, 2 * HEAD_DIM)),
        resident((1, D_RET)),
        resident((1, D_MLSTM)),
        layer_weight(d_mix, d),
    ]
    head_tiles = lambda heads, dtype: pltpu.VMEM((n_chunks, heads, HEAD_DIM, CHUNK), dtype)
    scratch_shapes = [
        pltpu.VMEM((rows, d), BF16),
        pltpu.VMEM((rows, Z_WIDTH), F32),
        pltpu.VMEM((rows + SUBLANES, 2 * D_MLSTM), F32),
        pltpu.VMEM((rows, LANES), F32),
        pltpu.VMEM((rows, d_mix), F32),
        pltpu.VMEM((rows, d_mix), BF16),
        pltpu.VMEM((rows, D_RET), BF16),
        pltpu.VMEM((rows, D_RET), BF16),
        pltpu.VMEM((rows, D_RET), BF16),
        pltpu.VMEM((rows, D_RET), BF16),
        head_tiles(RET_HEADS, BF16),
        pltpu.VMEM((rows, D_MLSTM), BF16),
        pltpu.VMEM((rows, D_MLSTM), BF16),
        pltpu.VMEM((rows, 2 * D_MLSTM), BF16),
        head_tiles(MLSTM_HEADS, F32),
        pltpu.VMEM((rows, D_MLSTM), F32),
        pltpu.VMEM((rows, D_MLSTM), F32),
        pltpu.VMEM((n_chunks, SUBLANES, CHUNK), F32),
        pltpu.VMEM((RET_HEADS, HEAD_DIM, HEAD_DIM), F32),
        pltpu.VMEM((MLSTM_HEADS, HEAD_DIM, 2 * HEAD_DIM), F32),
        pltpu.VMEM((MLSTM_HEADS, 1, LANES), F32),
    ]
    return pl.pallas_call(
        _mixer_kernel,
        grid=(b, s // rows),
        in_specs=in_specs,
        out_specs=tile(d),
        out_shape=jax.ShapeDtypeStruct(x.shape, x.dtype),
        scratch_shapes=scratch_shapes,
        compiler_params=pltpu.CompilerParams(
            dimension_semantics=("arbitrary", "arbitrary"), vmem_limit_bytes=VMEM_LIMIT_BYTES),
        name="mixer",
    )(x, mod, g.reshape(1, d), w_main, w_gate, gbias, cos, sin, conv_w, conv_b.reshape(1, -1),
      decay, w_q, w_k, cdec, tril, avg, g_ret.reshape(1, -1), g_ml.reshape(1, -1), w_out)


@jax.jit
def kernel(x, c, positions, norm_g, w_ada, b_ada, w_ff1, w_ff3, w_ff2, w_in, conv_w, conv_b,
           b_igate, b_fgate, g_ret_norm, g_mlstm_norm, w_out, g_final):
    depth = w_in.shape[0]
    ffn_rows, mix_rows = _tiles(x.shape[1])
    mods = _ada_mod(c, w_ada, b_ada)
    cos, sin = _rope_tables(positions, mix_rows)
    ffn_weights = (w_ff1.astype(BF16), w_ff3.astype(BF16), w_ff2.astype(BF16))
    w_main, w_gate, w_out = _mixer_weights(w_in, w_out)
    for l in range(depth):
        x = _ffn(x, mods[l, 0], norm_g[l, 0], (l, 0), *ffn_weights, ffn_rows)
        x = _mixer(x, mods[l, 1], norm_g[l, 1], l, w_main, w_gate, conv_w[l], conv_b[l],
                   b_igate[l], b_fgate[l], g_ret_norm[l], g_mlstm_norm[l], w_out, cos, sin,
                   mix_rows)
        x = _ffn(x, mods[l, 2], norm_g[l, 2], (l, 1), *ffn_weights, ffn_rows,
                 g_final=g_final if l == depth - 1 else None)
    return x
```

```python
import functools

import jax
import jax.numpy as jnp
from jax import lax
from jax.experimental import pallas as pl
from jax.experimental.pallas import tpu as pltpu

F32 = jnp.float32
BF16 = jnp.bfloat16

RET_HEADS = 4
MLSTM_HEADS = 4
HEAD_DIM = 128
CHUNK = 128
CONV_WIDTH = 4
ROPE_BASE = 10000.0
EPS = 1e-6
N_SUBLAYERS = 3
D_RET = RET_HEADS * HEAD_DIM
D_MLSTM = MLSTM_HEADS * HEAD_DIM

LANES = 128
SUBLANES = 8
MXU_COLS = 256
VMEM_LIMIT_BYTES = 56 * 1024 * 1024

COL_RQ = 0
COL_RK = D_RET
COL_RV = 2 * D_RET
COL_RG = 3 * D_RET
COL_MQK = 4 * D_RET
COL_MV = COL_MQK + 2 * D_MLSTM
COL_MO = COL_MV + D_MLSTM
COL_GATES = COL_MO + D_MLSTM

NT_DIMS = (((1,), (1,)), ((), ()))


def _tiles(seq):
    ffn_rows = min(512, seq)
    mix_rows = min(512, seq)
    assert seq % ffn_rows == 0 and seq % mix_rows == 0 and mix_rows % CHUNK == 0
    return ffn_rows, mix_rows


def _dot(a, b):
    return jnp.dot(a, b, preferred_element_type=F32)


def _rmsnorm(x, g):
    return x * lax.rsqrt(jnp.mean(x * x, axis=-1, keepdims=True) + EPS) * g


def _modulated_norm(x, mod, g):
    shift, scale = mod[0:1], mod[1:2]
    return _rmsnorm(x, g) * (1.0 + scale) + shift


def _log_sigmoid(x):
    return jnp.minimum(x, 0.0) - jnp.log1p(jnp.exp(-jnp.abs(x)))


def _fold_to_tile(t):
    acc = None
    for r0 in range(0, t.shape[0], SUBLANES):
        for c0 in range(0, t.shape[1], LANES):
            blk = t[r0:r0 + SUBLANES, c0:c0 + LANES]
            acc = blk if acc is None else acc + blk
    return acc


def _ada_kernel(c_ref, w_ref, b_ref, o_ref):
    sc = jax.nn.silu(c_ref[...])
    o_ref[...] = jnp.dot(sc, w_ref[...], preferred_element_type=F32,
                         precision=lax.Precision.HIGHEST) + b_ref[...]


def _ada_mod(c, w_ada, b_ada):
    depth, nsub, d, d3 = w_ada.shape
    n = depth * nsub
    b = c.shape[0]
    out = pl.pallas_call(
        _ada_kernel,
        grid=(n, d3 // d),
        in_specs=[
            pl.BlockSpec((b, d), lambda i, j: (0, 0)),
            pl.BlockSpec((None, d, d), lambda i, j: (i, 0, j)),
            pl.BlockSpec((None, 1, d), lambda i, j: (i, 0, j)),
        ],
        out_specs=pl.BlockSpec((None, b, d), lambda i, j: (i, 0, j)),
        out_shape=jax.ShapeDtypeStruct((n, b, d3), F32),
        compiler_params=pltpu.CompilerParams(
            dimension_semantics=("arbitrary", "arbitrary"), vmem_limit_bytes=VMEM_LIMIT_BYTES),
        name="ada_mod",
    )(c, w_ada.reshape(n, d, d3), b_ada.reshape(n, 1, d3))
    return out.reshape(depth, nsub, b, d3 // d, d)


def _rope_kernel(pos_ref, invf_ref, cos_ref, sin_ref):
    rows, half = pos_ref.shape[0], HEAD_DIM // 2
    pos = pos_ref[...].astype(F32)
    low = lax.broadcasted_iota(jnp.int32, (rows // 2, HEAD_DIM), 1) < half
    ang = jnp.where(low, pos[:rows // 2], pos[rows // 2:]) * invf_ref[...]
    c, s = jnp.cos(ang), jnp.sin(ang)
    c_swapped, s_swapped = pltpu.roll(c, half, 1), pltpu.roll(s, half, 1)
    cos_ref[:rows // 2, :] = jnp.where(low, c, c_swapped)
    cos_ref[rows // 2:, :] = jnp.where(low, c_swapped, c)
    sin_ref[:rows // 2, :] = jnp.where(low, -s, s_swapped)
    sin_ref[rows // 2:, :] = jnp.where(low, -s_swapped, s)


def _rope_tables(positions, rows):
    b, s = positions.shape
    assert rows % (2 * SUBLANES) == 0
    inv_freq = ROPE_BASE ** (-jnp.arange(0, HEAD_DIM, 2, dtype=F32) / HEAD_DIM)
    invf = jnp.concatenate([inv_freq, inv_freq]).reshape(1, HEAD_DIM)
    table = jax.ShapeDtypeStruct((b, s, HEAD_DIM), F32)
    const = pl.BlockSpec((1, HEAD_DIM), lambda i, j: (0, 0))
    tile = pl.BlockSpec((None, rows, HEAD_DIM), lambda i, j: (i, j, 0))
    return pl.pallas_call(
        _rope_kernel,
        grid=(b, s // rows),
        in_specs=[pl.BlockSpec((None, rows, 1), lambda i, j: (i, j, 0)), const],
        out_specs=[tile, tile],
        out_shape=[table, table],
        compiler_params=pltpu.CompilerParams(
            dimension_semantics=("arbitrary", "arbitrary"), vmem_limit_bytes=VMEM_LIMIT_BYTES),
        name="rope_tables",
    )(positions.reshape(b, s, 1), invf)


NORM_PIECES = 8


def _ffn_kernel(x_ref, xnext_ref, mod_ref, modnext_ref, g_ref, w1_ref, w3_ref, w2_ref, *rest,
                final_norm):
    if final_norm:
        gf_ref, o_ref, h_even_ref, h_odd_ref, act_ref = rest
    else:
        o_ref, h_even_ref, h_odd_ref, act_ref = rest
    t = pl.program_id(0)
    rows = x_ref.shape[0]
    d_ff = w1_ref.shape[1]

    def normalise(src_ref, src_mod_ref, dst_ref, r0, n):
        hn = _modulated_norm(src_ref[r0:r0 + n, :], src_mod_ref[...], g_ref[...])
        dst_ref[r0:r0 + n, :] = hn.astype(BF16)
        return hn

    @pl.when(t == 0)
    def _first_tile():
        normalise(x_ref, mod_ref, h_even_ref, 0, rows)

    def tile_step(h_ref, hnext_ref):
        piece = rows // NORM_PIECES
        pieces = list(range(0, rows, piece))
        anchor = None
        for c0 in range(0, d_ff, MXU_COLS):
            h = h_ref[...]
            a = _dot(h, w1_ref[:, c0:c0 + MXU_COLS])
            b = _dot(h, w3_ref[:, c0:c0 + MXU_COLS])
            act = jax.nn.silu(a) * b
            act_ref[:, c0:c0 + MXU_COLS] = act.astype(BF16)
            if anchor is not None:
                packed = 2 * SUBLANES
                act_ref[0:packed, c0:c0 + LANES] = (
                    act[0:packed, 0:LANES]
                    + 0.0 * jnp.concatenate([anchor, anchor], axis=0)).astype(BF16)
                anchor = None
            if pieces:
                anchor = _fold_to_tile(
                    normalise(xnext_ref, modnext_ref, hnext_ref, pieces.pop(0), piece))
        assert not pieces and anchor is None
        out = x_ref[...] + 0.5 * mod_ref[2:3, :] * _dot(act_ref[...], w2_ref[...])
        if final_norm:
            out = _rmsnorm(out, gf_ref[...])
        o_ref[...] = out

    @pl.when(t % 2 == 0)
    def _even_tile():
        tile_step(h_even_ref, h_odd_ref)

    @pl.when(t % 2 == 1)
    def _odd_tile():
        tile_step(h_odd_ref, h_even_ref)


def _ffn(x, mod, g, which, w1, w3, w2, rows, g_final=None):
    b, s, d = x.shape
    d_ff = w1.shape[-1]
    assert d_ff % MXU_COLS == 0 and rows % (NORM_PIECES * SUBLANES) == 0
    final_norm = g_final is not None
    tiles_per_seq = s // rows
    n_tiles = b * tiles_per_seq
    following = lambda t: jnp.minimum(t + 1, n_tiles - 1)
    resident = lambda shape: pl.BlockSpec(shape, lambda t: (0, 0), pipeline_mode=pl.Buffered(1))
    weight = lambda rows_, cols_: pl.BlockSpec((None, None, rows_, cols_),
                                               lambda t: (*which, 0, 0),
                                               pipeline_mode=pl.Buffered(1))
    tile = pl.BlockSpec((None, rows, d), lambda t: (t, 0, 0))
    in_specs = [
        tile,
        pl.BlockSpec((None, rows, d), lambda t: (following(t), 0, 0)),
        pl.BlockSpec((None, 3, d), lambda t: (t // tiles_per_seq, 0, 0)),
        pl.BlockSpec((None, 3, d), lambda t: (following(t) // tiles_per_seq, 0, 0)),
        resident((1, d)),
        weight(d, d_ff),
        weight(d, d_ff),
        weight(d_ff, d),
    ]
    x_tiles = x.reshape(n_tiles, rows, d)
    args = [x_tiles, x_tiles, mod, mod, g.reshape(1, d), w1, w3, w2]
    if final_norm:
        in_specs.append(resident((1, d)))
        args.append(g_final.reshape(1, d))
    out = pl.pallas_call(
        functools.partial(_ffn_kernel, final_norm=final_norm),
        grid=(n_tiles,),
        in_specs=in_specs,
        out_specs=tile,
        out_shape=jax.ShapeDtypeStruct(x_tiles.shape, x.dtype),
        scratch_shapes=[pltpu.VMEM((rows, d), BF16), pltpu.VMEM((rows, d), BF16),
                        pltpu.VMEM((rows, d_ff), BF16)],
        compiler_params=pltpu.CompilerParams(
            dimension_semantics=("arbitrary",), vmem_limit_bytes=VMEM_LIMIT_BYTES),
        name="ffn_final" if final_norm else "ffn",
    )(*args)
    return out.reshape(x.shape)


Z_RQ = 0
Z_RK = D_RET
Z_RG = 2 * D_RET
Z_MO = 3 * D_RET
Z_WIDTH = 3 * D_RET + D_MLSTM


def _row_mean_lanes(t, avg):
    hi = t.astype(BF16)
    lo = (t - hi.astype(F32)).astype(BF16)
    return _dot(hi, avg) + _dot(lo, avg)


def _prefix_max_rows(t, row_id):
    shift = 1
    while shift < CHUNK:
        t = jnp.maximum(t, jnp.where(row_id >= shift, pltpu.roll(t, shift, 0), -jnp.inf))
        shift *= 2
    return t


def _stage_retention(r, hh, z_ref, cos, sin, wq_ref, wk_ref, rq_ref, rqw_ref, rk_ref, rkwt_ref):
    r0, c0 = r * CHUNK, hh * HEAD_DIM
    q = z_ref[r0:r0 + CHUNK, Z_RQ + c0:Z_RQ + c0 + HEAD_DIM]
    k = z_ref[r0:r0 + CHUNK, Z_RK + c0:Z_RK + c0 + HEAD_DIM]
    half = HEAD_DIM // 2
    qr = q * cos + pltpu.roll(q, half, 1) * sin
    kr = (k * cos + pltpu.roll(k, half, 1) * sin) * (HEAD_DIM ** -0.5)
    rq_ref[r0:r0 + CHUNK, c0:c0 + HEAD_DIM] = qr.astype(BF16)
    rqw_ref[r0:r0 + CHUNK, c0:c0 + HEAD_DIM] = (qr * wq_ref[hh]).astype(BF16)
    rk_ref[r0:r0 + CHUNK, c0:c0 + HEAD_DIM] = kr.astype(BF16)
    rkwt_ref[r, hh] = (kr * wk_ref[hh]).T.astype(BF16)


def _stage_mlstm_qk(r, blk, qkraw_ref, convw_ref, convb_ref, mq_ref, mk_ref, mkt_ref):
    r0, c0 = r * CHUNK, blk * HEAD_DIM
    acc = convb_ref[:, c0:c0 + HEAD_DIM]
    for j in range(CONV_WIDTH):
        start = SUBLANES - (CONV_WIDTH - 1) + j + r0
        acc = acc + (convw_ref[j:j + 1, c0:c0 + HEAD_DIM]
                     * qkraw_ref[start:start + CHUNK, c0:c0 + HEAD_DIM])
    qk = jax.nn.silu(acc)
    if blk < MLSTM_HEADS:
        mq_ref[r0:r0 + CHUNK, c0:c0 + HEAD_DIM] = (qk * (HEAD_DIM ** -0.5)).astype(BF16)
    else:
        hh = blk - MLSTM_HEADS
        mk_ref[r0:r0 + CHUNK, hh * HEAD_DIM:(hh + 1) * HEAD_DIM] = qk.astype(BF16)
        mkt_ref[r, hh] = qk.T


def _stage_gates(r, gate_ref, tril_ref, row_id, cmb_ref, bb_ref, arow_ref):
    r0 = r * CHUNK
    gates = gate_ref[r0:r0 + CHUNK, :]
    bcum = jnp.dot(tril_ref[...], gates, preferred_element_type=F32,
                   precision=lax.Precision.HIGHEST)
    a = gates - pltpu.roll(bcum, LANES - MLSTM_HEADS, 1)
    cmax = _prefix_max_rows(a, row_id)
    arow_ref[r] = a.T[0:SUBLANES, :]
    for hh in range(MLSTM_HEADS):
        cols = slice(hh * HEAD_DIM, (hh + 1) * HEAD_DIM)
        cmb_ref[r0:r0 + CHUNK, cols] = jnp.broadcast_to(cmax[:, hh:hh + 1], (CHUNK, HEAD_DIM))
        fcol = MLSTM_HEADS + hh
        bb_ref[r0:r0 + CHUNK, cols] = jnp.broadcast_to(bcum[:, fcol:fcol + 1], (CHUNK, HEAD_DIM))


def _retention_chunk(hh, r, rq_ref, rqw_ref, rk_ref, rv_ref, rkwt_ref, decay_ref, cdec_ref,
                     state_ref, pre_ref):
    rows = slice(r * CHUNK, (r + 1) * CHUNK)
    cols = slice(hh * HEAD_DIM, (hh + 1) * HEAD_DIM)
    scores = lax.dot_general(rq_ref[rows, cols], rk_ref[rows, cols], NT_DIMS,
                             preferred_element_type=F32)
    yield
    vb = rv_ref[rows, cols]
    state = state_ref[hh]
    intra = _dot((scores * decay_ref[hh]).astype(BF16), vb)
    inter = _dot(rqw_ref[rows, cols], state.astype(BF16))
    update = _dot(rkwt_ref[r, hh], vb)
    yield
    pre_ref[rows, cols] = intra + inter
    state_ref[hh] = state * cdec_ref[hh] + update


def _mlstm_chunk(hh, r, causal, mq_ref, mk_ref, mv_ref, mkt_ref, cmb_ref, bb_ref, arow_ref,
                 c_ref, m_ref, pre_ref):
    rows = slice(r * CHUNK, (r + 1) * CHUNK)
    c0 = hh * HEAD_DIM
    cols = slice(c0, c0 + HEAD_DIM)
    qb = mq_ref[rows, cols]
    scores = lax.dot_general(qb, mk_ref[rows, cols], NT_DIMS, preferred_element_type=F32)
    yield
    v_aug = mv_ref[rows, 2 * c0:2 * c0 + 2 * HEAD_DIM]
    cmax = cmb_ref[rows, cols]
    b_l = bb_ref[rows, cols]
    a_row = arow_ref[r][hh:hh + 1, :]
    m_prev = m_ref[hh]
    state = c_ref[hh]
    mx = jnp.maximum(cmax, m_prev)
    w_d = jnp.exp(jnp.where(causal, a_row - mx, -jnp.inf))
    intra = _dot((scores * w_d).astype(BF16), v_aug)
    inter = _dot(qb, state.astype(BF16))
    b_last = b_l[CHUNK - 1:CHUNK, :]
    m_new = jnp.maximum(b_last + m_prev, b_last + cmax[CHUNK - 1:CHUNK, :])
    w_g = jnp.exp(b_last + a_row - m_new)
    update = _dot((mkt_ref[r, hh] * w_g).astype(BF16), v_aug)
    yield
    w_inter = jnp.exp(m_prev - mx)
    tot = intra + jnp.concatenate([w_inter, w_inter], axis=1) * inter
    num, den = tot[:, :HEAD_DIM], tot[:, HEAD_DIM:]
    h = num / jnp.maximum(jnp.abs(den), jnp.exp(-(b_l + mx)))
    pre_ref[rows, D_RET + c0:D_RET + c0 + HEAD_DIM] = h
    w_c = jnp.exp(b_last + m_prev - m_new)
    c_ref[hh] = jnp.concatenate([w_c, w_c], axis=1) * state + update
    m_ref[hh] = m_new


def _run_staggered(tasks, stages, fillers):
    n = len(tasks)
    n_steps = n + stages - 1
    for step in range(n_steps):
        for run_filler in fillers.get(step, ()):
            run_filler()
        for s in range(stages):
            j = step - s
            if 0 <= j < n:
                next(tasks[j], None)
    assert all(step < n_steps for step in fillers)


def _mixer_kernel(x_ref, mod_ref, g_ref, win_ref, wgate_ref, gbias_ref, cos_ref, sin_ref, convw_ref,
                  convb_ref,
                  decay_ref, wq_ref, wk_ref, cdec_ref, tril_ref, avg_ref, gret_ref, gml_ref, wout_ref,
                  o_ref,
                  h_ref, z_ref, qkraw_ref, gate_ref, pre_ref, y_ref,
                  rq_ref, rqw_ref, rk_ref, rv_ref, rkwt_ref, mq_ref, mk_ref, mv_ref, mkt_ref,
                  cmb_ref, bb_ref, arow_ref, sret_ref, c_ref, m_ref):
    rows_total = x_ref.shape[0]
    n_chunks = rows_total // CHUNK

    @pl.when(pl.program_id(1) == 0)
    def _start_of_sequence():
        sret_ref[...] = jnp.zeros_like(sret_ref)
        c_ref[...] = jnp.zeros_like(c_ref)
        m_ref[...] = jnp.zeros_like(m_ref)
        qkraw_ref[0:SUBLANES, :] = jnp.zeros((SUBLANES, qkraw_ref.shape[1]), F32)
        for hh in range(MLSTM_HEADS):
            ones_cols = slice((2 * hh + 1) * HEAD_DIM, (2 * hh + 2) * HEAD_DIM)
            mv_ref[:, ones_cols] = jnp.ones((rows_total, HEAD_DIM), BF16)

    mod = mod_ref[...]
    h_ref[...] = _modulated_norm(x_ref[...], mod, g_ref[...]).astype(BF16)

    def project(c0, width=D_RET):
        return _dot(h_ref[...], win_ref[:, c0:c0 + width])

    row_id = lax.broadcasted_iota(jnp.int32, (CHUNK, CHUNK), 0)
    col_id = lax.broadcasted_iota(jnp.int32, (CHUNK, CHUNK), 1)
    causal = row_id >= col_id

    def project_to_z(col, zcol, width=D_RET):
        z_ref[:, zcol:zcol + width] = project(col, width)

    def project_qk(c0):
        qkraw_ref[SUBLANES:SUBLANES + rows_total, c0:c0 + D_RET] = project(COL_MQK + c0)

    def project_gates():
        zg = _dot(h_ref[...], wgate_ref[...]) + gbias_ref[...]
        lane = lax.broadcasted_iota(jnp.int32, zg.shape, 1)
        gate_ref[...] = jnp.where(lane < MLSTM_HEADS, zg, _log_sigmoid(zg))

    def project_rv():
        rv_ref[...] = project(COL_RV).astype(BF16)

    def project_mv():
        mv = project(COL_MV)
        for hh in range(MLSTM_HEADS):
            mv_ref[:, 2 * hh * HEAD_DIM:(2 * hh + 1) * HEAD_DIM] = (
                mv[:, hh * HEAD_DIM:(hh + 1) * HEAD_DIM].astype(BF16))

    def stage_retention(r):
        cos = cos_ref[r * CHUNK:(r + 1) * CHUNK, :]
        sin = sin_ref[r * CHUNK:(r + 1) * CHUNK, :]
        for hh in range(RET_HEADS):
            _stage_retention(r, hh, z_ref, cos, sin, wq_ref, wk_ref, rq_ref, rqw_ref, rk_ref,
                             rkwt_ref)

    def stage_mlstm_q(r):
        for blk in range(MLSTM_HEADS):
            _stage_mlstm_qk(r, blk, qkraw_ref, convw_ref, convb_ref, mq_ref, mk_ref, mkt_ref)

    def stage_mlstm_k(r):
        for blk in range(MLSTM_HEADS, 2 * MLSTM_HEADS):
            _stage_mlstm_qk(r, blk, qkraw_ref, convw_ref, convb_ref, mq_ref, mk_ref, mkt_ref)

    def stage_gates(r):
        _stage_gates(r, gate_ref, tril_ref, row_id, cmb_ref, bb_ref, arow_ref)

    chunks = range(n_chunks)
    half = (n_chunks + 1) // 2
    emission = [
        ([functools.partial(project_qk, 0)], []),
        ([functools.partial(project_qk, D_RET)], [(stage_mlstm_q, r) for r in chunks[:half]]),
        ([project_gates, functools.partial(project_to_z, COL_RQ, Z_RQ)],
         [(stage_mlstm_q, r) for r in chunks[half:]]),
        ([functools.partial(project_to_z, COL_RK, Z_RK)],
         [(stage_mlstm_k, r) for r in chunks[:half]]),
        ([project_rv], [(stage_mlstm_k, r) for r in chunks[half:]] + [(stage_gates, r) for r in chunks]),
        ([project_mv], [(stage_retention, 0)]),
    ]
    for run_projections, run_stages in emission:
        for run_projection in run_projections:
            run_projection()
        for stage, r in run_stages:
            stage(r)
    qkraw_ref[0:SUBLANES, :] = qkraw_ref[rows_total:rows_total + SUBLANES, :]

    tasks = []
    for r in range(n_chunks):
        for hh in range(max(RET_HEADS, MLSTM_HEADS)):
            if hh < RET_HEADS:
                tasks.append(_retention_chunk(hh, r, rq_ref, rqw_ref, rk_ref, rv_ref, rkwt_ref,
                                              decay_ref, cdec_ref, sret_ref, pre_ref))
            if hh < MLSTM_HEADS:
                tasks.append(_mlstm_chunk(hh, r, causal, mq_ref, mk_ref, mv_ref, mkt_ref,
                                          cmb_ref, bb_ref, arow_ref, c_ref, m_ref, pre_ref))
    n_stages = 3
    n_steps = len(tasks) + n_stages - 1
    tasks_per_chunk = len(tasks) // n_chunks
    fillers = {}
    for r in range(1, n_chunks):
        fillers.setdefault((r - 1) * tasks_per_chunk, []).append(
            functools.partial(stage_retention, r))
    late = [(col + c0, zcol + c0) for col, zcol in ((COL_RG, Z_RG), (COL_MO, Z_MO))
            for c0 in range(0, D_RET, MXU_COLS)]
    for k, (col, zcol) in enumerate(late):
        fillers.setdefault(1 + k * (n_steps // len(late)), []).append(
            functools.partial(project_to_z, col, zcol, MXU_COLS))
    _run_staggered(tasks, n_stages, fillers)

    pair = 2 * HEAD_DIM
    avg = avg_ref[...]
    for p0 in range(0, D_RET + D_MLSTM, pair):
        pre = pre_ref[:, p0:p0 + pair]
        if p0 >= D_RET:
            m0 = Z_MO + p0 - D_RET
            pre = jax.nn.sigmoid(z_ref[:, m0:m0 + pair]) * pre
        d = pre - _row_mean_lanes(pre, avg)
        var = _row_mean_lanes(d * d, avg)
        if p0 < D_RET:
            hn = (jax.nn.silu(z_ref[:, Z_RG + p0:Z_RG + p0 + pair])
                  * (d * lax.rsqrt(var + EPS) * gret_ref[:, p0:p0 + pair]))
        else:
            hn = d * lax.rsqrt(var + EPS) * gml_ref[:, p0 - D_RET:p0 - D_RET + pair]
        y_ref[:, p0:p0 + pair] = hn.astype(y_ref.dtype)
    o_ref[...] = x_ref[...] + mod[2:3] * _dot(y_ref[...], wout_ref[...])


def _retention_constants():
    heads = jnp.arange(RET_HEADS, dtype=F32)
    log_gamma = jnp.log(1.0 - 2.0 ** (-5.0 - heads))
    idx = jnp.arange(CHUNK)
    diff = (idx[:, None] - idx[None, :]).astype(F32)
    decay = jnp.where(diff >= 0, jnp.exp(log_gamma[:, None, None] * jnp.maximum(diff, 0.0)), 0.0)
    w_k = jnp.exp(log_gamma[:, None] * (CHUNK - 1 - idx).astype(F32))
    w_q = jnp.exp(log_gamma[:, None] * (idx + 1).astype(F32))
    chunk_decay = jnp.exp(log_gamma * CHUNK)
    bcast = lambda t: jnp.broadcast_to(t[:, :, None], (RET_HEADS, CHUNK, HEAD_DIM))
    cdec = jnp.broadcast_to(chunk_decay[:, None, None], (RET_HEADS, 1, HEAD_DIM))
    return decay, bcast(w_q), bcast(w_k), cdec


def _mixer_weights(w_in, w_out):
    n_gates = w_in.shape[-1] - COL_GATES
    w_main = w_in[:, :, :COL_GATES].astype(BF16)
    w_gate = jnp.pad(w_in[:, :, COL_GATES:], ((0, 0), (0, 0), (0, LANES - n_gates))).astype(BF16)
    return w_main, w_gate, w_out.astype(BF16)


def _mixer(x, mod, g, layer, w_main, w_gate, conv_w, conv_b, b_igate, b_fgate, g_ret, g_ml, w_out,
           cos, sin, rows):
    b, s, d = x.shape
    d_mix = D_RET + D_MLSTM
    n_gates = 2 * MLSTM_HEADS
    n_chunks = rows // CHUNK
    assert w_main.shape[1:] == (d, COL_GATES) and w_out.shape[1:] == (d_mix, d)
    gbias = jnp.pad(jnp.concatenate([b_igate, b_fgate]), (0, LANES - n_gates)).reshape(1, LANES)
    decay, w_q, w_k, cdec = _retention_constants()
    tril = jnp.tril(jnp.ones((CHUNK, CHUNK), F32))
    head_of = jnp.arange(2 * HEAD_DIM) // HEAD_DIM
    avg = jnp.where(head_of[:, None] == head_of[None, :], 1.0 / HEAD_DIM, 0.0).astype(BF16)

    def resident(shape):
        zeros = (0,) * len(shape)
        return pl.BlockSpec(shape, lambda i, j: zeros, pipeline_mode=pl.Buffered(1))

    def layer_weight(rows_, cols_):
        return pl.BlockSpec((None, rows_, cols_), lambda i, j: (layer, 0, 0),
                            pipeline_mode=pl.Buffered(1))

    tile = lambda width: pl.BlockSpec((None, rows, width), lambda i, j: (i, j, 0))
    in_specs = [
        tile(d),
        pl.BlockSpec((None, 3, d), lambda i, j: (i, 0, 0)),
        resident((1, d)),
        layer_weight(d, COL_GATES),
        layer_weight(d, LANES),
        resident((1, LANES)),
        tile(HEAD_DIM),
        tile(HEAD_DIM),
        resident((CONV_WIDTH, 2 * D_MLSTM)),
        resident((1, 2 * D_MLSTM)),
        resident((RET_HEADS, CHUNK, CHUNK)),
        resident((RET_HEADS, CHUNK, HEAD_DIM)),
        resident((RET_HEADS, CHUNK, HEAD_DIM)),
        resident((RET_HEADS, 1, HEAD_DIM)),
        resident((CHUNK, CHUNK)),
        resident((2 * HEAD_DIM, 2 * HEAD_DIM)),
        resident((1, D_RET)),
        resident((1, D_MLSTM)),
        layer_weight(d_mix, d),
    ]
    head_tiles = lambda heads, dtype: pltpu.VMEM((n_chunks, heads, HEAD_DIM, CHUNK), dtype)
    scratch_shapes = [
        pltpu.VMEM((rows, d), BF16),
        pltpu.VMEM((rows, Z_WIDTH), F32),
        pltpu.VMEM((rows + SUBLANES, 2 * D_MLSTM), F32),
        pltpu.VMEM((rows, LANES), F32),
        pltpu.VMEM((rows, d_mix), F32),
        pltpu.VMEM((rows, d_mix), BF16),
        pltpu.VMEM((rows, D_RET), BF16),
        pltpu.VMEM((rows, D_RET), BF16),
        pltpu.VMEM((rows, D_RET), BF16),
        pltpu.VMEM((rows, D_RET), BF16),
        head_tiles(RET_HEADS, BF16),
        pltpu.VMEM((rows, D_MLSTM), BF16),
        pltpu.VMEM((rows, D_MLSTM), BF16),
        pltpu.VMEM((rows, 2 * D_MLSTM), BF16),
        head_tiles(MLSTM_HEADS, F32),
        pltpu.VMEM((rows, D_MLSTM), F32),
        pltpu.VMEM((rows, D_MLSTM), F32),
        pltpu.VMEM((n_chunks, SUBLANES, CHUNK), F32),
        pltpu.VMEM((RET_HEADS, HEAD_DIM, HEAD_DIM), F32),
        pltpu.VMEM((MLSTM_HEADS, HEAD_DIM, 2 * HEAD_DIM), F32),
        pltpu.VMEM((MLSTM_HEADS, 1, LANES), F32),
    ]
    return pl.pallas_call(
        _mixer_kernel,
        grid=(b, s // rows),
        in_specs=in_specs,
        out_specs=tile(d),
        out_shape=jax.ShapeDtypeStruct(x.shape, x.dtype),
        scratch_shapes=scratch_shapes,
        compiler_params=pltpu.CompilerParams(
            dimension_semantics=("arbitrary", "arbitrary"), vmem_limit_bytes=VMEM_LIMIT_BYTES),
        name="mixer",
    )(x, mod, g.reshape(1, d), w_main, w_gate, gbias, cos, sin, conv_w, conv_b.reshape(1, -1),
      decay, w_q, w_k, cdec, tril, avg, g_ret.reshape(1, -1), g_ml.reshape(1, -1), w_out)


@jax.jit
def kernel(x, c, positions, norm_g, w_ada, b_ada, w_ff1, w_ff3, w_ff2, w_in, conv_w, conv_b,
           b_igate, b_fgate, g_ret_norm, g_mlstm_norm, w_out, g_final):
    depth = w_in.shape[0]
    ffn_rows, mix_rows = _tiles(x.shape[1])
    mods = _ada_mod(c, w_ada, b_ada)
    cos, sin = _rope_tables(positions, mix_rows)
    ffn_weights = (w_ff1.astype(BF16), w_ff3.astype(BF16), w_ff2.astype(BF16))
    w_main, w_gate, w_out = _mixer_weights(w_in, w_out)
    for l in range(depth):
        x = _ffn(x, mods[l, 0], norm_g[l, 0], (l, 0), *ffn_weights, ffn_rows)
        x = _mixer(x, mods[l, 1], norm_g[l, 1], l, w_main, w_gate, conv_w[l], conv_b[l],
                   b_igate[l], b_fgate[l], g_ret_norm[l], g_mlstm_norm[l], w_out, cos, sin,
                   mix_rows)
        x = _ffn(x, mods[l, 2], norm_g[l, 2], (l, 1), *ffn_weights, ffn_rows,
                 g_final=g_final if l == depth - 1 else None)
    return x
```

```python
import functools

import jax
import jax.numpy as jnp
from jax import lax
from jax.experimental import pallas as pl
from jax.experimental.pallas import tpu as pltpu

F32 = jnp.float32
BF16 = jnp.bfloat16

RET_HEADS = 4
MLSTM_HEADS = 4
HEAD_DIM = 128
CHUNK = 128
CONV_WIDTH = 4
ROPE_BASE = 10000.0
EPS = 1e-6
N_SUBLAYERS = 3
D_RET = RET_HEADS * HEAD_DIM
D_MLSTM = MLSTM_HEADS * HEAD_DIM

LANES = 128
SUBLANES = 8
MXU_COLS = 256
VMEM_LIMIT_BYTES = 56 * 1024 * 1024

COL_RQ = 0
COL_RK = D_RET
COL_RV = 2 * D_RET
COL_RG = 3 * D_RET
COL_MQK = 4 * D_RET
COL_MV = COL_MQK + 2 * D_MLSTM
COL_MO = COL_MV + D_MLSTM
COL_GATES = COL_MO + D_MLSTM

NT_DIMS = (((1,), (1,)), ((), ()))


def _tiles(seq):
    ffn_rows = min(1024, seq)
    mix_rows = min(512, seq)
    assert seq % ffn_rows == 0 and seq % mix_rows == 0 and mix_rows % CHUNK == 0
    return ffn_rows, mix_rows


def _dot(a, b):
    return jnp.dot(a, b, preferred_element_type=F32)


def _rmsnorm(x, g):
    return x * lax.rsqrt(jnp.mean(x * x, axis=-1, keepdims=True) + EPS) * g


def _modulated_norm(x, mod, g):
    shift, scale = mod[0:1], mod[1:2]
    return _rmsnorm(x, g) * (1.0 + scale) + shift


def _log_sigmoid(x):
    return jnp.minimum(x, 0.0) - jnp.log1p(jnp.exp(-jnp.abs(x)))


def _fold_to_tile(t):
    acc = None
    for r0 in range(0, t.shape[0], SUBLANES):
        for c0 in range(0, t.shape[1], LANES):
            blk = t[r0:r0 + SUBLANES, c0:c0 + LANES]
            acc = blk if acc is None else acc + blk
    return acc


def _ada_kernel(c_ref, w_ref, b_ref, o_ref):
    sc = jax.nn.silu(c_ref[...])
    o_ref[...] = jnp.dot(sc, w_ref[...], preferred_element_type=F32,
                         precision=lax.Precision.HIGHEST) + b_ref[...]


def _ada_mod(c, w_ada, b_ada):
    depth, nsub, d, d3 = w_ada.shape
    n = depth * nsub
    b = c.shape[0]
    out = pl.pallas_call(
        _ada_kernel,
        grid=(n, d3 // d),
        in_specs=[
            pl.BlockSpec((b, d), lambda i, j: (0, 0)),
            pl.BlockSpec((None, d, d), lambda i, j: (i, 0, j)),
            pl.BlockSpec((None, 1, d), lambda i, j: (i, 0, j)),
        ],
        out_specs=pl.BlockSpec((None, b, d), lambda i, j: (i, 0, j)),
        out_shape=jax.ShapeDtypeStruct((n, b, d3), F32),
        compiler_params=pltpu.CompilerParams(
            dimension_semantics=("arbitrary", "arbitrary"), vmem_limit_bytes=VMEM_LIMIT_BYTES),
        name="ada_mod",
    )(c, w_ada.reshape(n, d, d3), b_ada.reshape(n, 1, d3))
    return out.reshape(depth, nsub, b, d3 // d, d)


def _rope_kernel(pos_ref, invf_ref, cos_ref, sin_ref):
    rows, half = pos_ref.shape[0], HEAD_DIM // 2
    pos = pos_ref[...].astype(F32)
    low = lax.broadcasted_iota(jnp.int32, (rows // 2, HEAD_DIM), 1) < half
    ang = jnp.where(low, pos[:rows // 2], pos[rows // 2:]) * invf_ref[...]
    c, s = jnp.cos(ang), jnp.sin(ang)
    c_swapped, s_swapped = pltpu.roll(c, half, 1), pltpu.roll(s, half, 1)
    cos_ref[:rows // 2, :] = jnp.where(low, c, c_swapped)
    cos_ref[rows // 2:, :] = jnp.where(low, c_swapped, c)
    sin_ref[:rows // 2, :] = jnp.where(low, -s, s_swapped)
    sin_ref[rows // 2:, :] = jnp.where(low, -s_swapped, s)


def _rope_tables(positions, rows):
    b, s = positions.shape
    assert rows % (2 * SUBLANES) == 0
    inv_freq = ROPE_BASE ** (-jnp.arange(0, HEAD_DIM, 2, dtype=F32) / HEAD_DIM)
    invf = jnp.concatenate([inv_freq, inv_freq]).reshape(1, HEAD_DIM)
    table = jax.ShapeDtypeStruct((b, s, HEAD_DIM), F32)
    const = pl.BlockSpec((1, HEAD_DIM), lambda i, j: (0, 0))
    tile = pl.BlockSpec((None, rows, HEAD_DIM), lambda i, j: (i, j, 0))
    return pl.pallas_call(
        _rope_kernel,
        grid=(b, s // rows),
        in_specs=[pl.BlockSpec((None, rows, 1), lambda i, j: (i, j, 0)), const],
        out_specs=[tile, tile],
        out_shape=[table, table],
        compiler_params=pltpu.CompilerParams(
            dimension_semantics=("arbitrary", "arbitrary"), vmem_limit_bytes=VMEM_LIMIT_BYTES),
        name="rope_tables",
    )(positions.reshape(b, s, 1), invf)


NORM_PIECES = 8


def _ffn_kernel(x_ref, xnext_ref, mod_ref, modnext_ref, g_ref, w1_ref, w3_ref, w2_ref, *rest,
                final_norm):
    if final_norm:
        gf_ref, o_ref, h_even_ref, h_odd_ref, act_ref = rest
    else:
        o_ref, h_even_ref, h_odd_ref, act_ref = rest
    t = pl.program_id(0)
    rows = x_ref.shape[0]
    d_ff = w1_ref.shape[1]

    def normalise(src_ref, src_mod_ref, dst_ref, r0, n):
        hn = _modulated_norm(src_ref[r0:r0 + n, :], src_mod_ref[...], g_ref[...])
        dst_ref[r0:r0 + n, :] = hn.astype(BF16)
        return hn

    @pl.when(t == 0)
    def _first_tile():
        normalise(x_ref, mod_ref, h_even_ref, 0, rows)

    def tile_step(h_ref, hnext_ref):
        piece = rows // NORM_PIECES
        pieces = list(range(0, rows, piece))
        anchor = None
        for c0 in range(0, d_ff, MXU_COLS):
            h = h_ref[...]
            a = _dot(h, w1_ref[:, c0:c0 + MXU_COLS])
            b = _dot(h, w3_ref[:, c0:c0 + MXU_COLS])
            act = jax.nn.silu(a) * b
            act_ref[:, c0:c0 + MXU_COLS] = act.astype(BF16)
            if anchor is not None:
                packed = 2 * SUBLANES
                act_ref[0:packed, c0:c0 + LANES] = (
                    act[0:packed, 0:LANES]
                    + 0.0 * jnp.concatenate([anchor, anchor], axis=0)).astype(BF16)
                anchor = None
            if pieces:
                anchor = _fold_to_tile(
                    normalise(xnext_ref, modnext_ref, hnext_ref, pieces.pop(0), piece))
        assert not pieces and anchor is None
        out = x_ref[...] + 0.5 * mod_ref[2:3, :] * _dot(act_ref[...], w2_ref[...])
        if final_norm:
            out = _rmsnorm(out, gf_ref[...])
        o_ref[...] = out

    @pl.when(t % 2 == 0)
    def _even_tile():
        tile_step(h_even_ref, h_odd_ref)

    @pl.when(t % 2 == 1)
    def _odd_tile():
        tile_step(h_odd_ref, h_even_ref)


def _ffn(x, mod, g, which, w1, w3, w2, rows, g_final=None):
    b, s, d = x.shape
    d_ff = w1.shape[-1]
    assert d_ff % MXU_COLS == 0 and rows % (NORM_PIECES * SUBLANES) == 0
    final_norm = g_final is not None
    tiles_per_seq = s // rows
    n_tiles = b * tiles_per_seq
    following = lambda t: jnp.minimum(t + 1, n_tiles - 1)
    resident = lambda shape: pl.BlockSpec(shape, lambda t: (0, 0), pipeline_mode=pl.Buffered(1))
    weight = lambda rows_, cols_: pl.BlockSpec((None, None, rows_, cols_),
                                               lambda t: (*which, 0, 0),
                                               pipeline_mode=pl.Buffered(1))
    tile = pl.BlockSpec((None, rows, d), lambda t: (t, 0, 0))
    in_specs = [
        tile,
        pl.BlockSpec((None, rows, d), lambda t: (following(t), 0, 0)),
        pl.BlockSpec((None, 3, d), lambda t: (t // tiles_per_seq, 0, 0)),
        pl.BlockSpec((None, 3, d), lambda t: (following(t) // tiles_per_seq, 0, 0)),
        resident((1, d)),
        weight(d, d_ff),
        weight(d, d_ff),
        weight(d_ff, d),
    ]
    x_tiles = x.reshape(n_tiles, rows, d)
    args = [x_tiles, x_tiles, mod, mod, g.reshape(1, d), w1, w3, w2]
    if final_norm:
        in_specs.append(resident((1, d)))
        args.append(g_final.reshape(1, d))
    out = pl.pallas_call(
        functools.partial(_ffn_kernel, final_norm=final_norm),
        grid=(n_tiles,),
        in_specs=in_specs,
        out_specs=tile,
        out_shape=jax.ShapeDtypeStruct(x_tiles.shape, x.dtype),
        scratch_shapes=[pltpu.VMEM((rows, d), BF16), pltpu.VMEM((rows, d), BF16),
                        pltpu.VMEM((rows, d_ff), BF16)],
        compiler_params=pltpu.CompilerParams(
            dimension_semantics=("arbitrary",), vmem_limit_bytes=VMEM_LIMIT_BYTES),
        name="ffn_final" if final_norm else "ffn",
    )(*args)
    return out.reshape(x.shape)


Z_RQ = 0
Z_RK = D_RET
Z_RG = 2 * D_RET
Z_MO = 3 * D_RET
Z_WIDTH = 3 * D_RET + D_MLSTM


def _row_mean_lanes(t, avg):
    hi = t.astype(BF16)
    lo = (t - hi.astype(F32)).astype(BF16)
    return _dot(hi, avg) + _dot(lo, avg)


def _prefix_scan_rows(t, row_id, combine, identity):
    shift = 1
    while shift < CHUNK:
        t = combine(t, jnp.where(row_id >= shift, pltpu.roll(t, shift, 0), identity))
        shift *= 2
    return t


def _stage_retention(r, hh, z_ref, cos, sin, wq_ref, wk_ref, rq_ref, rqw_ref, rk_ref, rkwt_ref):
    r0, c0 = r * CHUNK, hh * HEAD_DIM
    q = z_ref[r0:r0 + CHUNK, Z_RQ + c0:Z_RQ + c0 + HEAD_DIM]
    k = z_ref[r0:r0 + CHUNK, Z_RK + c0:Z_RK + c0 + HEAD_DIM]
    half = HEAD_DIM // 2
    qr = q * cos + pltpu.roll(q, half, 1) * sin
    kr = (k * cos + pltpu.roll(k, half, 1) * sin) * (HEAD_DIM ** -0.5)
    rq_ref[r0:r0 + CHUNK, c0:c0 + HEAD_DIM] = qr.astype(BF16)
    rqw_ref[r0:r0 + CHUNK, c0:c0 + HEAD_DIM] = (qr * wq_ref[hh]).astype(BF16)
    rk_ref[r0:r0 + CHUNK, c0:c0 + HEAD_DIM] = kr.astype(BF16)
    rkwt_ref[r, hh] = (kr * wk_ref[hh]).T.astype(BF16)


def _stage_mlstm_qk(r, blk, qkraw_ref, convw_ref, convb_ref, mq_ref, mk_ref, mkt_ref):
    r0, c0 = r * CHUNK, blk * HEAD_DIM
    acc = convb_ref[:, c0:c0 + HEAD_DIM]
    for j in range(CONV_WIDTH):
        start = SUBLANES - (CONV_WIDTH - 1) + j + r0
        acc = acc + (convw_ref[j:j + 1, c0:c0 + HEAD_DIM]
                     * qkraw_ref[start:start + CHUNK, c0:c0 + HEAD_DIM])
    qk = jax.nn.silu(acc)
    if blk < MLSTM_HEADS:
        mq_ref[r0:r0 + CHUNK, c0:c0 + HEAD_DIM] = (qk * (HEAD_DIM ** -0.5)).astype(BF16)
    else:
        hh = blk - MLSTM_HEADS
        mk_ref[r0:r0 + CHUNK, hh * HEAD_DIM:(hh + 1) * HEAD_DIM] = qk.astype(BF16)
        mkt_ref[r, hh] = qk.T


def _stage_gates(r, gate_ref, row_id, cmb_ref, bb_ref, arow_ref):
    r0 = r * CHUNK
    gates = gate_ref[r0:r0 + CHUNK, :]
    bcum = _prefix_scan_rows(gates, row_id, jnp.add, 0.0)
    a = gates - pltpu.roll(bcum, LANES - MLSTM_HEADS, 1)
    cmax = _prefix_scan_rows(a, row_id, jnp.maximum, -jnp.inf)
    arow_ref[r] = a.T[0:SUBLANES, :]
    for hh in range(MLSTM_HEADS):
        cols = slice(hh * HEAD_DIM, (hh + 1) * HEAD_DIM)
        cmb_ref[r0:r0 + CHUNK, cols] = jnp.broadcast_to(cmax[:, hh:hh + 1], (CHUNK, HEAD_DIM))
        fcol = MLSTM_HEADS + hh
        bb_ref[r0:r0 + CHUNK, cols] = jnp.broadcast_to(bcum[:, fcol:fcol + 1], (CHUNK, HEAD_DIM))


def _retention_chunk(hh, r, rq_ref, rqw_ref, rk_ref, rv_ref, rkwt_ref, decay_ref, cdec_ref,
                     state_ref, pre_ref):
    rows = slice(r * CHUNK, (r + 1) * CHUNK)
    cols = slice(hh * HEAD_DIM, (hh + 1) * HEAD_DIM)
    scores = lax.dot_general(rq_ref[rows, cols], rk_ref[rows, cols], NT_DIMS,
                             preferred_element_type=F32)
    yield
    vb = rv_ref[rows, cols]
    state = state_ref[hh]
    intra = _dot((scores * decay_ref[hh]).astype(BF16), vb)
    inter = _dot(rqw_ref[rows, cols], state.astype(BF16))
    update = _dot(rkwt_ref[r, hh], vb)
    yield
    pre_ref[rows, cols] = intra + inter
    state_ref[hh] = state * cdec_ref[hh] + update


def _mlstm_chunk(hh, r, causal, mq_ref, mk_ref, mv_ref, mkt_ref, cmb_ref, bb_ref, arow_ref,
                 c_ref, m_ref, pre_ref):
    rows = slice(r * CHUNK, (r + 1) * CHUNK)
    c0 = hh * HEAD_DIM
    cols = slice(c0, c0 + HEAD_DIM)
    qb = mq_ref[rows, cols]
    scores = lax.dot_general(qb, mk_ref[rows, cols], NT_DIMS, preferred_element_type=F32)
    yield
    v_aug = mv_ref[rows, 2 * c0:2 * c0 + 2 * HEAD_DIM]
    cmax = cmb_ref[rows, cols]
    b_l = bb_ref[rows, cols]
    a_row = arow_ref[r][hh:hh + 1, :]
    m_prev = m_ref[hh]
    state = c_ref[hh]
    mx = jnp.maximum(cmax, m_prev)
    w_d = jnp.exp(jnp.where(causal, a_row - mx, -jnp.inf))
    intra = _dot((scores * w_d).astype(BF16), v_aug)
    inter = _dot(qb, state.astype(BF16))
    b_last = b_l[CHUNK - 1:CHUNK, :]
    m_new = jnp.maximum(b_last + m_prev, b_last + cmax[CHUNK - 1:CHUNK, :])
    w_g = jnp.exp(b_last + a_row - m_new)
    update = _dot((mkt_ref[r, hh] * w_g).astype(BF16), v_aug)
    yield
    w_inter = jnp.exp(m_prev - mx)
    tot = intra + jnp.concatenate([w_inter, w_inter], axis=1) * inter
    num, den = tot[:, :HEAD_DIM], tot[:, HEAD_DIM:]
    h = num / jnp.maximum(jnp.abs(den), jnp.exp(-(b_l + mx)))
    pre_ref[rows, D_RET + c0:D_RET + c0 + HEAD_DIM] = h
    w_c = jnp.exp(b_last + m_prev - m_new)
    c_ref[hh] = jnp.concatenate([w_c, w_c], axis=1) * state + update
    m_ref[hh] = m_new


def _run_staggered(tasks, stages, fillers):
    n = len(tasks)
    n_steps = n + stages - 1
    for step in range(n_steps):
        for run_filler in fillers.get(step, ()):
            run_filler()
        for s in range(stages):
            j = step - s
            if 0 <= j < n:
                next(tasks[j], None)
    assert all(step < n_steps for step in fillers)


def _mixer_kernel(x_ref, mod_ref, g_ref, win_ref, wgate_ref, gbias_ref, cos_ref, sin_ref, convw_ref,
                  convb_ref,
                  decay_ref, wq_ref, wk_ref, cdec_ref, avg_ref, gret_ref, gml_ref, wout_ref,
                  o_ref,
                  h_ref, z_ref, qkraw_ref, gate_ref, pre_ref, y_ref,
                  rq_ref, rqw_ref, rk_ref, rv_ref, rkwt_ref, mq_ref, mk_ref, mv_ref, mkt_ref,
                  cmb_ref, bb_ref, arow_ref, sret_ref, c_ref, m_ref):
    rows_total = x_ref.shape[0]
    n_chunks = rows_total // CHUNK

    @pl.when(pl.program_id(1) == 0)
    def _start_of_sequence():
        sret_ref[...] = jnp.zeros_like(sret_ref)
        c_ref[...] = jnp.zeros_like(c_ref)
        m_ref[...] = jnp.zeros_like(m_ref)
        qkraw_ref[0:SUBLANES, :] = jnp.zeros((SUBLANES, qkraw_ref.shape[1]), F32)
        for hh in range(MLSTM_HEADS):
            ones_cols = slice((2 * hh + 1) * HEAD_DIM, (2 * hh + 2) * HEAD_DIM)
            mv_ref[:, ones_cols] = jnp.ones((rows_total, HEAD_DIM), BF16)

    mod = mod_ref[...]
    h_ref[...] = _modulated_norm(x_ref[...], mod, g_ref[...]).astype(BF16)

    def project(c0, width=D_RET):
        return _dot(h_ref[...], win_ref[:, c0:c0 + width])

    row_id = lax.broadcasted_iota(jnp.int32, (CHUNK, CHUNK), 0)
    col_id = lax.broadcasted_iota(jnp.int32, (CHUNK, CHUNK), 1)
    causal = row_id >= col_id

    def project_to_z(col, zcol, width=D_RET):
        z_ref[:, zcol:zcol + width] = project(col, width)

    def project_qk(c0):
        qkraw_ref[SUBLANES:SUBLANES + rows_total, c0:c0 + D_RET] = project(COL_MQK + c0)

    def project_gates():
        zg = _dot(h_ref[...], wgate_ref[...]) + gbias_ref[...]
        lane = lax.broadcasted_iota(jnp.int32, zg.shape, 1)
        gate_ref[...] = jnp.where(lane < MLSTM_HEADS, zg, _log_sigmoid(zg))

    def project_rv():
        rv_ref[...] = project(COL_RV).astype(BF16)

    def project_mv():
        mv = project(COL_MV)
        for hh in range(MLSTM_HEADS):
            mv_ref[:, 2 * hh * HEAD_DIM:(2 * hh + 1) * HEAD_DIM] = (
                mv[:, hh * HEAD_DIM:(hh + 1) * HEAD_DIM].astype(BF16))

    def stage_retention(r):
        cos = cos_ref[r * CHUNK:(r + 1) * CHUNK, :]
        sin = sin_ref[r * CHUNK:(r + 1) * CHUNK, :]
        for hh in range(RET_HEADS):
            _stage_retention(r, hh, z_ref, cos, sin, wq_ref, wk_ref, rq_ref, rqw_ref, rk_ref,
                             rkwt_ref)

    def stage_mlstm_q(r):
        for blk in range(MLSTM_HEADS):
            _stage_mlstm_qk(r, blk, qkraw_ref, convw_ref, convb_ref, mq_ref, mk_ref, mkt_ref)

    def stage_mlstm_k(r):
        for blk in range(MLSTM_HEADS, 2 * MLSTM_HEADS):
            _stage_mlstm_qk(r, blk, qkraw_ref, convw_ref, convb_ref, mq_ref, mk_ref, mkt_ref)

    def stage_gates(r):
        _stage_gates(r, gate_ref, row_id, cmb_ref, bb_ref, arow_ref)

    chunks = range(n_chunks)
    half = (n_chunks + 1) // 2
    emission = [
        ([functools.partial(project_qk, 0)], []),
        ([functools.partial(project_qk, D_RET)], [(stage_mlstm_q, r) for r in chunks[:half]]),
        ([project_gates, functools.partial(project_to_z, COL_RQ, Z_RQ)],
         [(stage_mlstm_q, r) for r in chunks[half:]]),
        ([functools.partial(project_to_z, COL_RK, Z_RK)],
         [(stage_mlstm_k, r) for r in chunks[:half]]),
        ([project_rv], [(stage_mlstm_k, r) for r in chunks[half:]] + [(stage_gates, r) for r in chunks]),
        ([project_mv], [(stage_retention, 0)]),
    ]
    for run_projections, run_stages in emission:
        for run_projection in run_projections:
            run_projection()
        for stage, r in run_stages:
            stage(r)
    qkraw_ref[0:SUBLANES, :] = qkraw_ref[rows_total:rows_total + SUBLANES, :]

    tasks = []
    for r in range(n_chunks):
        for hh in range(max(RET_HEADS, MLSTM_HEADS)):
            if hh < RET_HEADS:
                tasks.append(_retention_chunk(hh, r, rq_ref, rqw_ref, rk_ref, rv_ref, rkwt_ref,
                                              decay_ref, cdec_ref, sret_ref, pre_ref))
            if hh < MLSTM_HEADS:
                tasks.append(_mlstm_chunk(hh, r, causal, mq_ref, mk_ref, mv_ref, mkt_ref,
                                          cmb_ref, bb_ref, arow_ref, c_ref, m_ref, pre_ref))
    n_stages = 3
    n_steps = len(tasks) + n_stages - 1
    tasks_per_chunk = len(tasks) // n_chunks
    fillers = {}
    for r in range(1, n_chunks):
        fillers.setdefault((r - 1) * tasks_per_chunk, []).append(
            functools.partial(stage_retention, r))
    late = [(col + c0, zcol + c0) for col, zcol in ((COL_RG, Z_RG), (COL_MO, Z_MO))
            for c0 in range(0, D_RET, MXU_COLS)]
    for k, (col, zcol) in enumerate(late):
        fillers.setdefault(1 + k * (n_steps // len(late)), []).append(
            functools.partial(project_to_z, col, zcol, MXU_COLS))
    _run_staggered(tasks, n_stages, fillers)

    pair = 2 * HEAD_DIM
    avg = avg_ref[...]
    for p0 in range(0, D_RET + D_MLSTM, pair):
        pre = pre_ref[:, p0:p0 + pair]
        if p0 >= D_RET:
            m0 = Z_MO + p0 - D_RET
            pre = jax.nn.sigmoid(z_ref[:, m0:m0 + pair]) * pre
        d = pre - _row_mean_lanes(pre, avg)
        var = _row_mean_lanes(d * d, avg)
        if p0 < D_RET:
            hn = (jax.nn.silu(z_ref[:, Z_RG + p0:Z_RG + p0 + pair])
                  * (d * lax.rsqrt(var + EPS) * gret_ref[:, p0:p0 + pair]))
        else:
            hn = d * lax.rsqrt(var + EPS) * gml_ref[:, p0 - D_RET:p0 - D_RET + pair]
        y_ref[:, p0:p0 + pair] = hn.astype(y_ref.dtype)
    o_ref[...] = x_ref[...] + mod[2:3] * _dot(y_ref[...], wout_ref[...])


def _retention_constants():
    heads = jnp.arange(RET_HEADS, dtype=F32)
    log_gamma = jnp.log(1.0 - 2.0 ** (-5.0 - heads))
    idx = jnp.arange(CHUNK)
    diff = (idx[:, None] - idx[None, :]).astype(F32)
    decay = jnp.where(diff >= 0, jnp.exp(log_gamma[:, None, None] * jnp.maximum(diff, 0.0)), 0.0)
    w_k = jnp.exp(log_gamma[:, None] * (CHUNK - 1 - idx).astype(F32))
    w_q = jnp.exp(log_gamma[:, None] * (idx + 1).astype(F32))
    chunk_decay = jnp.exp(log_gamma * CHUNK)
    bcast = lambda t: jnp.broadcast_to(t[:, :, None], (RET_HEADS, CHUNK, HEAD_DIM))
    cdec = jnp.broadcast_to(chunk_decay[:, None, None], (RET_HEADS, 1, HEAD_DIM))
    return decay, bcast(w_q), bcast(w_k), cdec


def _mixer_weights(w_in, w_out):
    n_gates = w_in.shape[-1] - COL_GATES
    w_main = w_in[:, :, :COL_GATES].astype(BF16)
    w_gate = jnp.pad(w_in[:, :, COL_GATES:], ((0, 0), (0, 0), (0, LANES - n_gates))).astype(BF16)
    return w_main, w_gate, w_out.astype(BF16)


def _mixer(x, mod, g, layer, w_main, w_gate, conv_w, conv_b, b_igate, b_fgate, g_ret, g_ml, w_out,
           cos, sin, rows):
    b, s, d = x.shape
    d_mix = D_RET + D_MLSTM
    n_gates = 2 * MLSTM_HEADS
    n_chunks = rows // CHUNK
    assert w_main.shape[1:] == (d, COL_GATES) and w_out.shape[1:] == (d_mix, d)
    gbias = jnp.pad(jnp.concatenate([b_igate, b_fgate]), (0, LANES - n_gates)).reshape(1, LANES)
    decay, w_q, w_k, cdec = _retention_constants()
    head_of = jnp.arange(2 * HEAD_DIM) // HEAD_DIM
    avg = jnp.where(head_of[:, None] == head_of[None, :], 1.0 / HEAD_DIM, 0.0).astype(BF16)

    def resident(shape):
        zeros = (0,) * len(shape)
        return pl.BlockSpec(shape, lambda i, j: zeros, pipeline_mode=pl.Buffered(1))

    def layer_weight(rows_, cols_):
        return pl.BlockSpec((None, rows_, cols_), lambda i, j: (layer, 0, 0),
                            pipeline_mode=pl.Buffered(1))

    tile = lambda width: pl.BlockSpec((None, rows, width), lambda i, j: (i, j, 0))
    in_specs = [
        tile(d),
        pl.BlockSpec((None, 3, d), lambda i, j: (i, 0, 0)),
        resident((1, d)),
        layer_weight(d, COL_GATES),
        layer_weight(d, LANES),
        resident((1, LANES)),
        tile(HEAD_DIM),
        tile(HEAD_DIM),
        resident((CONV_WIDTH, 2 * D_MLSTM)),
        resident((1, 2 * D_MLSTM)),
        resident((RET_HEADS, CHUNK, CHUNK)),
        resident((RET_HEADS, CHUNK, HEAD_DIM)),
        resident((RET_HEADS, CHUNK, HEAD_DIM)),
        resident((RET_HEADS, 1, HEAD_DIM)),
        resident((2 * HEAD_DIM, 2 * HEAD_DIM)),
        resident((1, D_RET)),
        resident((1, D_MLSTM)),
        layer_weight(d_mix, d),
    ]
    head_tiles = lambda heads, dtype: pltpu.VMEM((n_chunks, heads, HEAD_DIM, CHUNK), dtype)
    scratch_shapes = [
        pltpu.VMEM((rows, d), BF16),
        pltpu.VMEM((rows, Z_WIDTH), F32),
        pltpu.VMEM((rows + SUBLANES, 2 * D_MLSTM), F32),
        pltpu.VMEM((rows, LANES), F32),
        pltpu.VMEM((rows, d_mix), F32),
        pltpu.VMEM((rows, d_mix), BF16),
        pltpu.VMEM((rows, D_RET), BF16),
        pltpu.VMEM((rows, D_RET), BF16),
        pltpu.VMEM((rows, D_RET), BF16),
        pltpu.VMEM((rows, D_RET), BF16),
        head_tiles(RET_HEADS, BF16),
        pltpu.VMEM((rows, D_MLSTM), BF16),
        pltpu.VMEM((rows, D_MLSTM), BF16),
        pltpu.VMEM((rows, 2 * D_MLSTM), BF16),
        head_tiles(MLSTM_HEADS, F32),
        pltpu.VMEM((rows, D_MLSTM), F32),
        pltpu.VMEM((rows, D_MLSTM), F32),
        pltpu.VMEM((n_chunks, SUBLANES, CHUNK), F32),
        pltpu.VMEM((RET_HEADS, HEAD_DIM, HEAD_DIM), F32),
        pltpu.VMEM((MLSTM_HEADS, HEAD_DIM, 2 * HEAD_DIM), F32),
        pltpu.VMEM((MLSTM_HEADS, 1, LANES), F32),
    ]
    return pl.pallas_call(
        _mixer_kernel,
        grid=(b, s // rows),
        in_specs=in_specs,
        out_specs=tile(d),
        out_shape=jax.ShapeDtypeStruct(x.shape, x.dtype),
        scratch_shapes=scratch_shapes,
        compiler_params=pltpu.CompilerParams(
            dimension_semantics=("arbitrary", "arbitrary"), vmem_limit_bytes=VMEM_LIMIT_BYTES),
        name="mixer",
    )(x, mod, g.reshape(1, d), w_main, w_gate, gbias, cos, sin, conv_w, conv_b.reshape(1, -1),
      decay, w_q, w_k, cdec, avg, g_ret.reshape(1, -1), g_ml.reshape(1, -1), w_out)


@jax.jit
def kernel(x, c, positions, norm_g, w_ada, b_ada, w_ff1, w_ff3, w_ff2, w_in, conv_w, conv_b,
           b_igate, b_fgate, g_ret_norm, g_mlstm_norm, w_out, g_final):
    depth = w_in.shape[0]
    ffn_rows, mix_rows = _tiles(x.shape[1])
    mods = _ada_mod(c, w_ada, b_ada)
    cos, sin = _rope_tables(positions, mix_rows)
    ffn_weights = (w_ff1.astype(BF16), w_ff3.astype(BF16), w_ff2.astype(BF16))
    w_main, w_gate, w_out = _mixer_weights(w_in, w_out)
    for l in range(depth):
        x = _ffn(x, mods[l, 0], norm_g[l, 0], (l, 0), *ffn_weights, ffn_rows)
        x = _mixer(x, mods[l, 1], norm_g[l, 1], l, w_main, w_gate, conv_w[l], conv_b[l],
                   b_igate[l], b_fgate[l], g_ret_norm[l], g_mlstm_norm[l], w_out, cos, sin,
                   mix_rows)
        x = _ffn(x, mods[l, 2], norm_g[l, 2], (l, 1), *ffn_weights, ffn_rows,
                 g_final=g_final if l == depth - 1 else None)
    return x
```

```python
import functools

import jax
import jax.numpy as jnp
from jax import lax
from jax.experimental import pallas as pl
from jax.experimental.pallas import tpu as pltpu

F32 = jnp.float32
BF16 = jnp.bfloat16

RET_HEADS = 4
MLSTM_HEADS = 4
HEAD_DIM = 128
CHUNK = 128
CONV_WIDTH = 4
ROPE_BASE = 10000.0
EPS = 1e-6
N_SUBLAYERS = 3
D_RET = RET_HEADS * HEAD_DIM
D_MLSTM = MLSTM_HEADS * HEAD_DIM

LANES = 128
SUBLANES = 8
MXU_COLS = 256
VMEM_LIMIT_BYTES = 56 * 1024 * 1024

COL_RQ = 0
COL_RK = D_RET
COL_RV = 2 * D_RET
COL_RG = 3 * D_RET
COL_MQK = 4 * D_RET
COL_MV = COL_MQK + 2 * D_MLSTM
COL_MO = COL_MV + D_MLSTM
COL_GATES = COL_MO + D_MLSTM

NT_DIMS = (((1,), (1,)), ((), ()))


def _tiles(seq):
    ffn_rows = min(1024, seq)
    mix_rows = min(512, seq)
    prepare_rows = seq // 2
    assert seq % ffn_rows == 0 and seq % mix_rows == 0 and mix_rows % CHUNK == 0
    return ffn_rows, mix_rows, prepare_rows


def _dot(a, b):
    return jnp.dot(a, b, preferred_element_type=F32)


def _rmsnorm(x, g):
    return x * lax.rsqrt(jnp.mean(x * x, axis=-1, keepdims=True) + EPS) * g


def _modulated_norm(x, mod, g):
    shift, scale = mod[0:1], mod[1:2]
    return _rmsnorm(x, g) * (1.0 + scale) + shift


def _log_sigmoid(x):
    return jnp.minimum(x, 0.0) - jnp.log1p(jnp.exp(-jnp.abs(x)))


def _fold_to_tile(t):
    acc = None
    for r0 in range(0, t.shape[0], SUBLANES):
        for c0 in range(0, t.shape[1], LANES):
            blk = t[r0:r0 + SUBLANES, c0:c0 + LANES]
            acc = blk if acc is None else acc + blk
    return acc


def _ada_kernel(c_ref, w_ref, b_ref, o_ref):
    sc = jax.nn.silu(c_ref[...])
    w = w_ref[...]
    s_hi = sc.astype(BF16)
    s_lo = (sc - s_hi.astype(F32)).astype(BF16)
    w_hi = w.astype(BF16)
    w_lo = (w - w_hi.astype(F32)).astype(BF16)
    n = sc.shape[0]
    both = _dot(jnp.concatenate([s_hi, s_lo], axis=0), w_hi)
    o_ref[...] = both[:n] + both[n:] + _dot(s_hi, w_lo) + b_ref[...]


def _ada_mod(c, w_ada, b_ada):
    depth, nsub, d, d3 = w_ada.shape
    n = depth * nsub
    b = c.shape[0]
    out = pl.pallas_call(
        _ada_kernel,
        grid=(n, d3 // d),
        in_specs=[
            pl.BlockSpec((b, d), lambda i, j: (0, 0)),
            pl.BlockSpec((None, d, d), lambda i, j: (i, 0, j)),
            pl.BlockSpec((None, 1, d), lambda i, j: (i, 0, j)),
        ],
        out_specs=pl.BlockSpec((None, b, d), lambda i, j: (i, 0, j)),
        out_shape=jax.ShapeDtypeStruct((n, b, d3), F32),
        compiler_params=pltpu.CompilerParams(
            dimension_semantics=("arbitrary", "arbitrary"), vmem_limit_bytes=VMEM_LIMIT_BYTES),
        name="ada_mod",
    )(c, w_ada.reshape(n, d, d3), b_ada.reshape(n, 1, d3))
    return out.reshape(depth, nsub, b, d3 // d, d)


def _prepare_kernel(pos_ref, invf_ref, *refs):
    n_weights = (len(refs) - 2) // 2
    weight_refs = refs[:n_weights]
    cos_ref, sin_ref = refs[n_weights:n_weights + 2]
    for src_ref, dst_ref in zip(weight_refs, refs[n_weights + 2:]):
        dst_ref[...] = src_ref[...].astype(BF16)
    rows, half = pos_ref.shape[0], HEAD_DIM // 2
    pos = pos_ref[...].astype(F32)
    low = lax.broadcasted_iota(jnp.int32, (rows // 2, HEAD_DIM), 1) < half
    ang = jnp.where(low, pos[:rows // 2], pos[rows // 2:]) * invf_ref[...]
    c, s = jnp.cos(ang), jnp.sin(ang)
    c_swapped, s_swapped = pltpu.roll(c, half, 1), pltpu.roll(s, half, 1)
    cos_ref[:rows // 2, :] = jnp.where(low, c, c_swapped)
    cos_ref[rows // 2:, :] = jnp.where(low, c_swapped, c)
    sin_ref[:rows // 2, :] = jnp.where(low, -s, s_swapped)
    sin_ref[rows // 2:, :] = jnp.where(low, -s_swapped, s)


def _prepare(positions, rows, weights):
    b, s = positions.shape
    assert rows % (2 * SUBLANES) == 0 and s % rows == 0
    n_j = s // rows
    n_steps = b * n_j
    inv_freq = ROPE_BASE ** (-jnp.arange(0, HEAD_DIM, 2, dtype=F32) / HEAD_DIM)
    invf = jnp.concatenate([inv_freq, inv_freq]).reshape(1, HEAD_DIM)
    table = jax.ShapeDtypeStruct((b, s, HEAD_DIM), F32)
    const = pl.BlockSpec((1, HEAD_DIM), lambda i, j: (0, 0))
    tile = pl.BlockSpec((None, rows, HEAD_DIM), lambda i, j: (i, j, 0))
    flat, weight_specs, cast_shapes = [], [], []
    for w, n_cols in weights:
        w2d = w.reshape(-1, w.shape[-1])
        block_rows = w2d.shape[0] // n_steps
        packed = 2 * SUBLANES
        assert block_rows * n_steps == w2d.shape[0] and block_rows % packed == 0
        assert n_cols % LANES == 0
        flat.append(w2d)
        weight_specs.append(pl.BlockSpec((block_rows, n_cols), lambda i, j: (i * n_j + j, 0)))
        cast_shapes.append(jax.ShapeDtypeStruct((w2d.shape[0], n_cols), BF16))
    outs = pl.pallas_call(
        _prepare_kernel,
        grid=(b, n_j),
        in_specs=[pl.BlockSpec((None, rows, 1), lambda i, j: (i, j, 0)), const] + weight_specs,
        out_specs=[tile, tile] + weight_specs,
        out_shape=[table, table] + cast_shapes,
        compiler_params=pltpu.CompilerParams(
            dimension_semantics=("arbitrary", "arbitrary"), vmem_limit_bytes=VMEM_LIMIT_BYTES),
        name="prepare",
    )(positions.reshape(b, s, 1), invf, *flat)
    casts = [o.reshape(*w.shape[:-1], n_cols) for o, (w, n_cols) in zip(outs[2:], weights)]
    return outs[0], outs[1], casts


NORM_PIECES = 8


def _ffn_kernel(x_ref, xnext_ref, mod_ref, modnext_ref, g_ref, w1_ref, w3_ref, w2_ref, *rest,
                final_norm):
    if final_norm:
        gf_ref, o_ref, h_even_ref, h_odd_ref, act_ref = rest
    else:
        o_ref, h_even_ref, h_odd_ref, act_ref = rest
    t = pl.program_id(0)
    rows = x_ref.shape[0]
    d_ff = w1_ref.shape[1]

    def normalise(src_ref, src_mod_ref, dst_ref, r0, n):
        hn = _modulated_norm(src_ref[r0:r0 + n, :], src_mod_ref[...], g_ref[...])
        dst_ref[r0:r0 + n, :] = hn.astype(BF16)
        return hn

    @pl.when(t == 0)
    def _first_tile():
        normalise(x_ref, mod_ref, h_even_ref, 0, rows)

    def tile_step(h_ref, hnext_ref):
        piece = rows // NORM_PIECES
        pieces = list(range(0, rows, piece))
        anchor = None
        for c0 in range(0, d_ff, MXU_COLS):
            h = h_ref[...]
            a = _dot(h, w1_ref[:, c0:c0 + MXU_COLS])
            b = _dot(h, w3_ref[:, c0:c0 + MXU_COLS])
            act = jax.nn.silu(a) * b
            act_ref[:, c0:c0 + MXU_COLS] = act.astype(BF16)
            if anchor is not None:
                packed = 2 * SUBLANES
                act_ref[0:packed, c0:c0 + LANES] = (
                    act[0:packed, 0:LANES]
                    + 0.0 * jnp.concatenate([anchor, anchor], axis=0)).astype(BF16)
                anchor = None
            if pieces:
                anchor = _fold_to_tile(
                    normalise(xnext_ref, modnext_ref, hnext_ref, pieces.pop(0), piece))
        assert not pieces and anchor is None
        out = x_ref[...] + 0.5 * mod_ref[2:3, :] * _dot(act_ref[...], w2_ref[...])
        if final_norm:
            out = _rmsnorm(out, gf_ref[...])
        o_ref[...] = out

    @pl.when(t % 2 == 0)
    def _even_tile():
        tile_step(h_even_ref, h_odd_ref)

    @pl.when(t % 2 == 1)
    def _odd_tile():
        tile_step(h_odd_ref, h_even_ref)


def _ffn(x, mod, g, which, w1, w3, w2, rows, g_final=None):
    b, s, d = x.shape
    d_ff = w1.shape[-1]
    assert d_ff % MXU_COLS == 0 and rows % (NORM_PIECES * SUBLANES) == 0
    final_norm = g_final is not None
    tiles_per_seq = s // rows
    n_tiles = b * tiles_per_seq
    following = lambda t: jnp.minimum(t + 1, n_tiles - 1)
    resident = lambda shape: pl.BlockSpec(shape, lambda t: (0, 0), pipeline_mode=pl.Buffered(1))
    weight = lambda rows_, cols_: pl.BlockSpec((None, None, rows_, cols_),
                                               lambda t: (*which, 0, 0),
                                               pipeline_mode=pl.Buffered(1))
    tile = pl.BlockSpec((None, rows, d), lambda t: (t, 0, 0))
    in_specs = [
        tile,
        pl.BlockSpec((None, rows, d), lambda t: (following(t), 0, 0)),
        pl.BlockSpec((None, 3, d), lambda t: (t // tiles_per_seq, 0, 0)),
        pl.BlockSpec((None, 3, d), lambda t: (following(t) // tiles_per_seq, 0, 0)),
        resident((1, d)),
        weight(d, d_ff),
        weight(d, d_ff),
        weight(d_ff, d),
    ]
    x_tiles = x.reshape(n_tiles, rows, d)
    args = [x_tiles, x_tiles, mod, mod, g.reshape(1, d), w1, w3, w2]
    if final_norm:
        in_specs.append(resident((1, d)))
        args.append(g_final.reshape(1, d))
    out = pl.pallas_call(
        functools.partial(_ffn_kernel, final_norm=final_norm),
        grid=(n_tiles,),
        in_specs=in_specs,
        out_specs=tile,
        out_shape=jax.ShapeDtypeStruct(x_tiles.shape, x.dtype),
        scratch_shapes=[pltpu.VMEM((rows, d), BF16), pltpu.VMEM((rows, d), BF16),
                        pltpu.VMEM((rows, d_ff), BF16)],
        compiler_params=pltpu.CompilerParams(
            dimension_semantics=("arbitrary",), vmem_limit_bytes=VMEM_LIMIT_BYTES),
        name="ffn_final" if final_norm else "ffn",
    )(*args)
    return out.reshape(x.shape)


Z_RQ = 0
Z_RK = D_RET
Z_RG = 2 * D_RET
Z_MO = 3 * D_RET
Z_WIDTH = 3 * D_RET + D_MLSTM


def _row_mean_lanes(t, avg):
    hi = t.astype(BF16)
    lo = (t - hi.astype(F32)).astype(BF16)
    return _dot(hi, avg) + _dot(lo, avg)


def _prefix_scan_rows(t, row_id, combine, identity):
    shift = 1
    while shift < CHUNK:
        t = combine(t, jnp.where(row_id >= shift, pltpu.roll(t, shift, 0), identity))
        shift *= 2
    return t


def _stage_retention(r, hh, z_ref, cos, sin, wq_ref, wk_ref, rq_ref, rqw_ref, rk_ref, rkwt_ref):
    r0, c0 = r * CHUNK, hh * HEAD_DIM
    q = z_ref[r0:r0 + CHUNK, Z_RQ + c0:Z_RQ + c0 + HEAD_DIM]
    k = z_ref[r0:r0 + CHUNK, Z_RK + c0:Z_RK + c0 + HEAD_DIM]
    half = HEAD_DIM // 2
    qr = q * cos + pltpu.roll(q, half, 1) * sin
    kr = (k * cos + pltpu.roll(k, half, 1) * sin) * (HEAD_DIM ** -0.5)
    rq_ref[r0:r0 + CHUNK, c0:c0 + HEAD_DIM] = qr.astype(BF16)
    rqw_ref[r0:r0 + CHUNK, c0:c0 + HEAD_DIM] = (qr * wq_ref[hh]).astype(BF16)
    rk_ref[r0:r0 + CHUNK, c0:c0 + HEAD_DIM] = kr.astype(BF16)
    rkwt_ref[r, hh] = (kr * wk_ref[hh]).T.astype(BF16)


def _stage_mlstm_qk(r, blk, qkraw_ref, convw_ref, convb_ref, mq_ref, mk_ref, mkt_ref):
    r0, c0 = r * CHUNK, blk * HEAD_DIM
    acc = convb_ref[:, c0:c0 + HEAD_DIM]
    for j in range(CONV_WIDTH):
        start = SUBLANES - (CONV_WIDTH - 1) + j + r0
        acc = acc + (convw_ref[j:j + 1, c0:c0 + HEAD_DIM]
                     * qkraw_ref[start:start + CHUNK, c0:c0 + HEAD_DIM])
    qk = jax.nn.silu(acc)
    if blk < MLSTM_HEADS:
        mq_ref[r0:r0 + CHUNK, c0:c0 + HEAD_DIM] = (qk * (HEAD_DIM ** -0.5)).astype(BF16)
    else:
        hh = blk - MLSTM_HEADS
        mk_ref[r0:r0 + CHUNK, hh * HEAD_DIM:(hh + 1) * HEAD_DIM] = qk.astype(BF16)
        mkt_ref[r, hh] = qk.T


def _stage_gates(r, gate_ref, row_id, cmb_ref, bb_ref, arow_ref):
    r0 = r * CHUNK
    gates = gate_ref[r0:r0 + CHUNK, :]
    bcum = _prefix_scan_rows(gates, row_id, jnp.add, 0.0)
    a = gates - pltpu.roll(bcum, LANES - MLSTM_HEADS, 1)
    cmax = _prefix_scan_rows(a, row_id, jnp.maximum, -jnp.inf)
    arow_ref[r] = a.T[0:SUBLANES, :]
    for hh in range(MLSTM_HEADS):
        cols = slice(hh * HEAD_DIM, (hh + 1) * HEAD_DIM)
        cmb_ref[r0:r0 + CHUNK, cols] = jnp.broadcast_to(cmax[:, hh:hh + 1], (CHUNK, HEAD_DIM))
        fcol = MLSTM_HEADS + hh
        bb_ref[r0:r0 + CHUNK, cols] = jnp.broadcast_to(bcum[:, fcol:fcol + 1], (CHUNK, HEAD_DIM))


def _retention_chunk(hh, r, rq_ref, rqw_ref, rk_ref, rv_ref, rkwt_ref, decay_ref, cdec_ref,
                     state_ref, pre_ref):
    rows = slice(r * CHUNK, (r + 1) * CHUNK)
    cols = slice(hh * HEAD_DIM, (hh + 1) * HEAD_DIM)
    scores = lax.dot_general(rq_ref[rows, cols], rk_ref[rows, cols], NT_DIMS,
                             preferred_element_type=F32)
    yield
    vb = rv_ref[rows, cols]
    state = state_ref[hh]
    intra = _dot((scores * decay_ref[hh]).astype(BF16), vb)
    inter = _dot(rqw_ref[rows, cols], state.astype(BF16))
    update = _dot(rkwt_ref[r, hh], vb)
    yield
    pre_ref[rows, cols] = intra + inter
    state_ref[hh] = state * cdec_ref[hh] + update


def _mlstm_chunk(hh, r, causal, mq_ref, mk_ref, mv_ref, mkt_ref, cmb_ref, bb_ref, arow_ref,
                 c_ref, m_ref, pre_ref):
    rows = slice(r * CHUNK, (r + 1) * CHUNK)
    c0 = hh * HEAD_DIM
    cols = slice(c0, c0 + HEAD_DIM)
    qb = mq_ref[rows, cols]
    scores = lax.dot_general(qb, mk_ref[rows, cols], NT_DIMS, preferred_element_type=F32)
    yield
    v_aug = mv_ref[rows, 2 * c0:2 * c0 + 2 * HEAD_DIM]
    cmax = cmb_ref[rows, cols]
    b_l = bb_ref[rows, cols]
    a_row = arow_ref[r][hh:hh + 1, :]
    m_prev = m_ref[hh]
    state = c_ref[hh]
    mx = jnp.maximum(cmax, m_prev)
    w_d = jnp.exp(jnp.where(causal, a_row - mx, -jnp.inf))
    intra = _dot((scores * w_d).astype(BF16), v_aug)
    inter = _dot(qb, state.astype(BF16))
    b_last = b_l[CHUNK - 1:CHUNK, :]
    m_new = jnp.maximum(b_last + m_prev, b_last + cmax[CHUNK - 1:CHUNK, :])
    w_g = jnp.exp(b_last + a_row - m_new)
    update = _dot((mkt_ref[r, hh] * w_g).astype(BF16), v_aug)
    yield
    w_inter = jnp.exp(m_prev - mx)
    tot = intra + jnp.concatenate([w_inter, w_inter], axis=1) * inter
    num, den = tot[:, :HEAD_DIM], tot[:, HEAD_DIM:]
    h = num / jnp.maximum(jnp.abs(den), jnp.exp(-(b_l + mx)))
    pre_ref[rows, D_RET + c0:D_RET + c0 + HEAD_DIM] = h
    w_c = jnp.exp(b_last + m_prev - m_new)
    c_ref[hh] = jnp.concatenate([w_c, w_c], axis=1) * state + update
    m_ref[hh] = m_new


def _run_staggered(tasks, stages, fillers):
    n = len(tasks)
    n_steps = n + stages - 1
    for step in range(n_steps):
        for run_filler in fillers.get(step, ()):
            run_filler()
        for s in range(stages):
            j = step - s
            if 0 <= j < n:
                next(tasks[j], None)
    assert all(step < n_steps for step in fillers)


def _mixer_kernel(x_ref, mod_ref, g_ref, win_ref, wgate_ref, gbias_ref, cos_ref, sin_ref, convw_ref,
                  convb_ref,
                  decay_ref, wq_ref, wk_ref, cdec_ref, avg_ref, gret_ref, gml_ref, wout_ref,
                  o_ref,
                  h_ref, z_ref, qkraw_ref, gate_ref, pre_ref, y_ref,
                  rq_ref, rqw_ref, rk_ref, rv_ref, rkwt_ref, mq_ref, mk_ref, mv_ref, mkt_ref,
                  cmb_ref, bb_ref, arow_ref, sret_ref, c_ref, m_ref):
    rows_total = x_ref.shape[0]
    n_chunks = rows_total // CHUNK

    @pl.when(pl.program_id(1) == 0)
    def _start_of_sequence():
        sret_ref[...] = jnp.zeros_like(sret_ref)
        c_ref[...] = jnp.zeros_like(c_ref)
        m_ref[...] = jnp.zeros_like(m_ref)
        qkraw_ref[0:SUBLANES, :] = jnp.zeros((SUBLANES, qkraw_ref.shape[1]), F32)
        for hh in range(MLSTM_HEADS):
            ones_cols = slice((2 * hh + 1) * HEAD_DIM, (2 * hh + 2) * HEAD_DIM)
            mv_ref[:, ones_cols] = jnp.ones((rows_total, HEAD_DIM), BF16)

    mod = mod_ref[...]
    h_ref[...] = _modulated_norm(x_ref[...], mod, g_ref[...]).astype(BF16)

    def project(c0, width=D_RET):
        return _dot(h_ref[...], win_ref[:, c0:c0 + width])

    row_id = lax.broadcasted_iota(jnp.int32, (CHUNK, CHUNK), 0)
    col_id = lax.broadcasted_iota(jnp.int32, (CHUNK, CHUNK), 1)
    causal = row_id >= col_id

    def project_to_z(col, zcol, width=D_RET):
        z_ref[:, zcol:zcol + width] = project(col, width)

    def project_qk(c0):
        qkraw_ref[SUBLANES:SUBLANES + rows_total, c0:c0 + D_RET] = project(COL_MQK + c0)

    def project_gates():
        zg = _dot(h_ref[...], wgate_ref[...]) + gbias_ref[...]
        lane = lax.broadcasted_iota(jnp.int32, zg.shape, 1)
        gate_ref[...] = jnp.where(lane < MLSTM_HEADS, zg, _log_sigmoid(zg))

    def project_rv():
        rv_ref[...] = project(COL_RV).astype(BF16)

    def project_mv():
        mv = project(COL_MV)
        for hh in range(MLSTM_HEADS):
            mv_ref[:, 2 * hh * HEAD_DIM:(2 * hh + 1) * HEAD_DIM] = (
                mv[:, hh * HEAD_DIM:(hh + 1) * HEAD_DIM].astype(BF16))

    def stage_retention(r):
        cos = cos_ref[r * CHUNK:(r + 1) * CHUNK, :]
        sin = sin_ref[r * CHUNK:(r + 1) * CHUNK, :]
        for hh in range(RET_HEADS):
            _stage_retention(r, hh, z_ref, cos, sin, wq_ref, wk_ref, rq_ref, rqw_ref, rk_ref,
                             rkwt_ref)

    def stage_mlstm_q(r):
        for blk in range(MLSTM_HEADS):
            _stage_mlstm_qk(r, blk, qkraw_ref, convw_ref, convb_ref, mq_ref, mk_ref, mkt_ref)

    def stage_mlstm_k(r):
        for blk in range(MLSTM_HEADS, 2 * MLSTM_HEADS):
            _stage_mlstm_qk(r, blk, qkraw_ref, convw_ref, convb_ref, mq_ref, mk_ref, mkt_ref)

    def stage_gates(r):
        _stage_gates(r, gate_ref, row_id, cmb_ref, bb_ref, arow_ref)

    chunks = range(n_chunks)
    half = (n_chunks + 1) // 2
    emission = [
        ([functools.partial(project_qk, 0)], []),
        ([functools.partial(project_qk, D_RET)], [(stage_mlstm_q, r) for r in chunks[:half]]),
        ([project_gates, functools.partial(project_to_z, COL_RQ, Z_RQ)],
         [(stage_mlstm_q, r) for r in chunks[half:]]),
        ([functools.partial(project_to_z, COL_RK, Z_RK)],
         [(stage_mlstm_k, r) for r in chunks[:half]]),
        ([project_rv], [(stage_mlstm_k, r) for r in chunks[half:]] + [(stage_gates, r) for r in chunks]),
        ([project_mv], [(stage_retention, 0)]),
    ]
    for run_projections, run_stages in emission:
        for run_projection in run_projections:
            run_projection()
        for stage, r in run_stages:
            stage(r)
    qkraw_ref[0:SUBLANES, :] = qkraw_ref[rows_total:rows_total + SUBLANES, :]

    tasks = []
    for r in range(n_chunks):
        for hh in range(max(RET_HEADS, MLSTM_HEADS)):
            if hh < MLSTM_HEADS:
                tasks.append(_mlstm_chunk(hh, r, causal, mq_ref, mk_ref, mv_ref, mkt_ref,
                                          cmb_ref, bb_ref, arow_ref, c_ref, m_ref, pre_ref))
            if hh < RET_HEADS:
                tasks.append(_retention_chunk(hh, r, rq_ref, rqw_ref, rk_ref, rv_ref, rkwt_ref,
                                              decay_ref, cdec_ref, sret_ref, pre_ref))
    n_stages = 3
    n_steps = len(tasks) + n_stages - 1
    tasks_per_chunk = len(tasks) // n_chunks
    fillers = {}
    for r in range(1, n_chunks):
        fillers.setdefault((r - 1) * tasks_per_chunk, []).append(
            functools.partial(stage_retention, r))
    late = [(col + c0, zcol + c0) for col, zcol in ((COL_RG, Z_RG), (COL_MO, Z_MO))
            for c0 in range(0, D_RET, MXU_COLS)]
    for k, (col, zcol) in enumerate(late):
        fillers.setdefault(1 + k * (n_steps // len(late)), []).append(
            functools.partial(project_to_z, col, zcol, MXU_COLS))
    _run_staggered(tasks, n_stages, fillers)

    pair = 2 * HEAD_DIM
    avg = avg_ref[...]
    for p0 in range(0, D_RET + D_MLSTM, pair):
        pre = pre_ref[:, p0:p0 + pair]
        if p0 >= D_RET:
            m0 = Z_MO + p0 - D_RET
            pre = jax.nn.sigmoid(z_ref[:, m0:m0 + pair]) * pre
        d = pre - _row_mean_lanes(pre, avg)
        var = _row_mean_lanes(d * d, avg)
        if p0 < D_RET:
            hn = (jax.nn.silu(z_ref[:, Z_RG + p0:Z_RG + p0 + pair])
                  * (d * lax.rsqrt(var + EPS) * gret_ref[:, p0:p0 + pair]))
        else:
            hn = d * lax.rsqrt(var + EPS) * gml_ref[:, p0 - D_RET:p0 - D_RET + pair]
        y_ref[:, p0:p0 + pair] = hn.astype(y_ref.dtype)
    o_ref[...] = x_ref[...] + mod[2:3] * _dot(y_ref[...], wout_ref[...])


def _retention_constants():
    heads = jnp.arange(RET_HEADS, dtype=F32)
    log_gamma = jnp.log(1.0 - 2.0 ** (-5.0 - heads))
    idx = jnp.arange(CHUNK)
    diff = (idx[:, None] - idx[None, :]).astype(F32)
    decay = jnp.where(diff >= 0, jnp.exp(log_gamma[:, None, None] * jnp.maximum(diff, 0.0)), 0.0)
    w_k = jnp.exp(log_gamma[:, None] * (CHUNK - 1 - idx).astype(F32))
    w_q = jnp.exp(log_gamma[:, None] * (idx + 1).astype(F32))
    chunk_decay = jnp.exp(log_gamma * CHUNK)
    bcast = lambda t: jnp.broadcast_to(t[:, :, None], (RET_HEADS, CHUNK, HEAD_DIM))
    cdec = jnp.broadcast_to(chunk_decay[:, None, None], (RET_HEADS, 1, HEAD_DIM))
    return decay, bcast(w_q), bcast(w_k), cdec


def _gate_weights(w_in):
    n_gates = w_in.shape[-1] - COL_GATES
    return jnp.pad(w_in[:, :, COL_GATES:], ((0, 0), (0, 0), (0, LANES - n_gates))).astype(BF16)


def _mixer(x, mod, g, layer, w_main, w_gate, conv_w, conv_b, b_igate, b_fgate, g_ret, g_ml, w_out,
           cos, sin, rows):
    b, s, d = x.shape
    d_mix = D_RET + D_MLSTM
    n_gates = 2 * MLSTM_HEADS
    n_chunks = rows // CHUNK
    assert w_main.shape[1:] == (d, COL_GATES) and w_out.shape[1:] == (d_mix, d)
    gbias = jnp.pad(jnp.concatenate([b_igate, b_fgate]), (0, LANES - n_gates)).reshape(1, LANES)
    decay, w_q, w_k, cdec = _retention_constants()
    head_of = jnp.arange(2 * HEAD_DIM) // HEAD_DIM
    avg = jnp.where(head_of[:, None] == head_of[None, :], 1.0 / HEAD_DIM, 0.0).astype(BF16)

    def resident(shape):
        zeros = (0,) * len(shape)
        return pl.BlockSpec(shape, lambda i, j: zeros, pipeline_mode=pl.Buffered(1))

    def layer_weight(rows_, cols_):
        return pl.BlockSpec((None, rows_, cols_), lambda i, j: (layer, 0, 0),
                            pipeline_mode=pl.Buffered(1))

    tile = lambda width: pl.BlockSpec((None, rows, width), lambda i, j: (i, j, 0))
    in_specs = [
        tile(d),
        pl.BlockSpec((None, 3, d), lambda i, j: (i, 0, 0)),
        resident((1, d)),
        layer_weight(d, COL_GATES),
        layer_weight(d, LANES),
        resident((1, LANES)),
        tile(HEAD_DIM),
        tile(HEAD_DIM),
        resident((CONV_WIDTH, 2 * D_MLSTM)),
        resident((1, 2 * D_MLSTM)),
        resident((RET_HEADS, CHUNK, CHUNK)),
        resident((RET_HEADS, CHUNK, HEAD_DIM)),
        resident((RET_HEADS, CHUNK, HEAD_DIM)),
        resident((RET_HEADS, 1, HEAD_DIM)),
        resident((2 * HEAD_DIM, 2 * HEAD_DIM)),
        resident((1, D_RET)),
        resident((1, D_MLSTM)),
        layer_weight(d_mix, d),
    ]
    head_tiles = lambda heads, dtype: pltpu.VMEM((n_chunks, heads, HEAD_DIM, CHUNK), dtype)
    scratch_shapes = [
        pltpu.VMEM((rows, d), BF16),
        pltpu.VMEM((rows, Z_WIDTH), F32),
        pltpu.VMEM((rows + SUBLANES, 2 * D_MLSTM), F32),
        pltpu.VMEM((rows, LANES), F32),
        pltpu.VMEM((rows, d_mix), F32),
        pltpu.VMEM((rows, d_mix), BF16),
        pltpu.VMEM((rows, D_RET), BF16),
        pltpu.VMEM((rows, D_RET), BF16),
        pltpu.VMEM((rows, D_RET), BF16),
        pltpu.VMEM((rows, D_RET), BF16),
        head_tiles(RET_HEADS, BF16),
        pltpu.VMEM((rows, D_MLSTM), BF16),
        pltpu.VMEM((rows, D_MLSTM), BF16),
        pltpu.VMEM((rows, 2 * D_MLSTM), BF16),
        head_tiles(MLSTM_HEADS, F32),
        pltpu.VMEM((rows, D_MLSTM), F32),
        pltpu.VMEM((rows, D_MLSTM), F32),
        pltpu.VMEM((n_chunks, SUBLANES, CHUNK), F32),
        pltpu.VMEM((RET_HEADS, HEAD_DIM, HEAD_DIM), F32),
        pltpu.VMEM((MLSTM_HEADS, HEAD_DIM, 2 * HEAD_DIM), F32),
        pltpu.VMEM((MLSTM_HEADS, 1, LANES), F32),
    ]
    return pl.pallas_call(
        _mixer_kernel,
        grid=(b, s // rows),
        in_specs=in_specs,
        out_specs=tile(d),
        out_shape=jax.ShapeDtypeStruct(x.shape, x.dtype),
        scratch_shapes=scratch_shapes,
        compiler_params=pltpu.CompilerParams(
            dimension_semantics=("arbitrary", "arbitrary"), vmem_limit_bytes=VMEM_LIMIT_BYTES),
        name="mixer",
    )(x, mod, g.reshape(1, d), w_main, w_gate, gbias, cos, sin, conv_w, conv_b.reshape(1, -1),
      decay, w_q, w_k, cdec, avg, g_ret.reshape(1, -1), g_ml.reshape(1, -1), w_out)


@jax.jit
def kernel(x, c, positions, norm_g, w_ada, b_ada, w_ff1, w_ff3, w_ff2, w_in, conv_w, conv_b,
           b_igate, b_fgate, g_ret_norm, g_mlstm_norm, w_out, g_final):
    depth = w_in.shape[0]
    ffn_rows, mix_rows, prepare_rows = _tiles(x.shape[1])
    mods = _ada_mod(c, w_ada, b_ada)
    d, d_ff = w_ff1.shape[-2:]
    cos, sin, (*ffn_weights, w_main, w_out) = _prepare(
        positions, prepare_rows,
        [(w_ff1, d_ff), (w_ff3, d_ff), (w_ff2, d), (w_in, COL_GATES), (w_out, d)])
    w_gate = _gate_weights(w_in)
    for l in range(depth):
        x = _ffn(x, mods[l, 0], norm_g[l, 0], (l, 0), *ffn_weights, ffn_rows)
        x = _mixer(x, mods[l, 1], norm_g[l, 1], l, w_main, w_gate, conv_w[l], conv_b[l],
                   b_igate[l], b_fgate[l], g_ret_norm[l], g_mlstm_norm[l], w_out, cos, sin,
                   mix_rows)
        x = _ffn(x, mods[l, 2], norm_g[l, 2], (l, 1), *ffn_weights, ffn_rows,
                 g_final=g_final if l == depth - 1 else None)
    return x
```

```python
import functools

import jax
import jax.numpy as jnp
from jax import lax
from jax.experimental import pallas as pl
from jax.experimental.pallas import tpu as pltpu

F32 = jnp.float32
BF16 = jnp.bfloat16

RET_HEADS = 4
MLSTM_HEADS = 4
HEAD_DIM = 128
CHUNK = 128
CONV_WIDTH = 4
ROPE_BASE = 10000.0
EPS = 1e-6
N_SUBLAYERS = 3
D_RET = RET_HEADS * HEAD_DIM
D_MLSTM = MLSTM_HEADS * HEAD_DIM

LANES = 128
SUBLANES = 8
MXU_COLS = 256
VMEM_LIMIT_BYTES = 56 * 1024 * 1024

COL_RQ = 0
COL_RK = D_RET
COL_RV = 2 * D_RET
COL_RG = 3 * D_RET
COL_MQK = 4 * D_RET
COL_MV = COL_MQK + 2 * D_MLSTM
COL_MO = COL_MV + D_MLSTM
COL_GATES = COL_MO + D_MLSTM

NT_DIMS = (((1,), (1,)), ((), ()))


def _tiles(seq):
    ffn_rows = min(1024, seq)
    mix_rows = min(512, seq)
    prepare_rows = seq // 2
    assert seq % ffn_rows == 0 and seq % mix_rows == 0 and mix_rows % CHUNK == 0
    return ffn_rows, mix_rows, prepare_rows


def _dot(a, b):
    return jnp.dot(a, b, preferred_element_type=F32)


def _rmsnorm(x, g):
    return x * lax.rsqrt(jnp.mean(x * x, axis=-1, keepdims=True) + EPS) * g


def _modulated_norm(x, mod, g):
    shift, scale = mod[0:1], mod[1:2]
    return _rmsnorm(x, g) * (1.0 + scale) + shift


def _log_sigmoid(x):
    return jnp.minimum(x, 0.0) - jnp.log1p(jnp.exp(-jnp.abs(x)))


def _fold_to_tile(t):
    acc = None
    for r0 in range(0, t.shape[0], SUBLANES):
        for c0 in range(0, t.shape[1], LANES):
            blk = t[r0:r0 + SUBLANES, c0:c0 + LANES]
            acc = blk if acc is None else acc + blk
    return acc


def _ada_kernel(c_ref, w_ref, b_ref, o_ref):
    sc = jax.nn.silu(c_ref[...])
    w = w_ref[...]
    s_hi = sc.astype(BF16)
    s_lo = (sc - s_hi.astype(F32)).astype(BF16)
    w_hi = w.astype(BF16)
    w_lo = (w - w_hi.astype(F32)).astype(BF16)
    n = sc.shape[0]
    both = _dot(jnp.concatenate([s_hi, s_lo], axis=0), w_hi)
    o_ref[...] = both[:n] + both[n:] + _dot(s_hi, w_lo) + b_ref[...]


def _ada_mod(c, w_ada, b_ada):
    depth, nsub, d, d3 = w_ada.shape
    n = depth * nsub
    b = c.shape[0]
    out = pl.pallas_call(
        _ada_kernel,
        grid=(n, d3 // d),
        in_specs=[
            pl.BlockSpec((b, d), lambda i, j: (0, 0)),
            pl.BlockSpec((None, d, d), lambda i, j: (i, 0, j)),
            pl.BlockSpec((None, 1, d), lambda i, j: (i, 0, j)),
        ],
        out_specs=pl.BlockSpec((None, b, d), lambda i, j: (i, 0, j)),
        out_shape=jax.ShapeDtypeStruct((n, b, d3), F32),
        compiler_params=pltpu.CompilerParams(
            dimension_semantics=("arbitrary", "arbitrary"), vmem_limit_bytes=VMEM_LIMIT_BYTES),
        name="ada_mod",
    )(c, w_ada.reshape(n, d, d3), b_ada.reshape(n, 1, d3))
    return out.reshape(depth, nsub, b, d3 // d, d)


def _prepare_kernel(pos_ref, invf_ref, *refs):
    n_weights = (len(refs) - 2) // 2
    weight_refs = refs[:n_weights]
    cos_ref, sin_ref = refs[n_weights:n_weights + 2]
    for src_ref, dst_ref in zip(weight_refs, refs[n_weights + 2:]):
        dst_ref[...] = src_ref[...].astype(BF16)
    rows, half = pos_ref.shape[0], HEAD_DIM // 2
    pos = pos_ref[...].astype(F32)
    low = lax.broadcasted_iota(jnp.int32, (rows // 2, HEAD_DIM), 1) < half
    ang = jnp.where(low, pos[:rows // 2], pos[rows // 2:]) * invf_ref[...]
    c, s = jnp.cos(ang), jnp.sin(ang)
    c_swapped, s_swapped = pltpu.roll(c, half, 1), pltpu.roll(s, half, 1)
    cos_ref[:rows // 2, :] = jnp.where(low, c, c_swapped)
    cos_ref[rows // 2:, :] = jnp.where(low, c_swapped, c)
    sin_ref[:rows // 2, :] = jnp.where(low, -s, s_swapped)
    sin_ref[rows // 2:, :] = jnp.where(low, -s_swapped, s)


def _prepare(positions, rows, weights):
    b, s = positions.shape
    assert rows % (2 * SUBLANES) == 0 and s % rows == 0
    n_j = s // rows
    n_steps = b * n_j
    inv_freq = ROPE_BASE ** (-jnp.arange(0, HEAD_DIM, 2, dtype=F32) / HEAD_DIM)
    invf = jnp.concatenate([inv_freq, inv_freq]).reshape(1, HEAD_DIM)
    table = jax.ShapeDtypeStruct((b, s, HEAD_DIM), F32)
    const = pl.BlockSpec((1, HEAD_DIM), lambda i, j: (0, 0))
    tile = pl.BlockSpec((None, rows, HEAD_DIM), lambda i, j: (i, j, 0))
    flat, weight_specs, cast_shapes = [], [], []
    for w, n_cols in weights:
        w2d = w.reshape(-1, w.shape[-1])
        block_rows = w2d.shape[0] // n_steps
        packed = 2 * SUBLANES
        assert block_rows * n_steps == w2d.shape[0] and block_rows % packed == 0
        assert n_cols % LANES == 0
        flat.append(w2d)
        weight_specs.append(pl.BlockSpec((block_rows, n_cols), lambda i, j: (i * n_j + j, 0)))
        cast_shapes.append(jax.ShapeDtypeStruct((w2d.shape[0], n_cols), BF16))
    outs = pl.pallas_call(
        _prepare_kernel,
        grid=(b, n_j),
        in_specs=[pl.BlockSpec((None, rows, 1), lambda i, j: (i, j, 0)), const] + weight_specs,
        out_specs=[tile, tile] + weight_specs,
        out_shape=[table, table] + cast_shapes,
        compiler_params=pltpu.CompilerParams(
            dimension_semantics=("arbitrary", "arbitrary"), vmem_limit_bytes=VMEM_LIMIT_BYTES),
        name="prepare",
    )(positions.reshape(b, s, 1), invf, *flat)
    casts = [o.reshape(*w.shape[:-1], n_cols) for o, (w, n_cols) in zip(outs[2:], weights)]
    return outs[0], outs[1], casts


NORM_PIECES = 8


def _ffn_kernel(x_ref, xnext_ref, mod_ref, modnext_ref, g_ref, w1_ref, w3_ref, w2_ref, *rest,
                final_norm):
    if final_norm:
        gf_ref, o_ref, h_even_ref, h_odd_ref, act_ref = rest
    else:
        o_ref, h_even_ref, h_odd_ref, act_ref = rest
    t = pl.program_id(0)
    rows = x_ref.shape[0]
    d_ff = w1_ref.shape[1]

    def normalise(src_ref, src_mod_ref, dst_ref, r0, n):
        hn = _modulated_norm(src_ref[r0:r0 + n, :], src_mod_ref[...], g_ref[...])
        dst_ref[r0:r0 + n, :] = hn.astype(BF16)
        return hn

    @pl.when(t == 0)
    def _first_tile():
        normalise(x_ref, mod_ref, h_even_ref, 0, rows)

    def tile_step(h_ref, hnext_ref):
        piece = rows // NORM_PIECES
        pieces = list(range(0, rows, piece))
        anchor = None
        for c0 in range(0, d_ff, MXU_COLS):
            h = h_ref[...]
            a = _dot(h, w1_ref[:, c0:c0 + MXU_COLS])
            b = _dot(h, w3_ref[:, c0:c0 + MXU_COLS])
            act = jax.nn.silu(a) * b
            act_ref[:, c0:c0 + MXU_COLS] = act.astype(BF16)
            if anchor is not None:
                packed = 2 * SUBLANES
                act_ref[0:packed, c0:c0 + LANES] = (
                    act[0:packed, 0:LANES]
                    + 0.0 * jnp.concatenate([anchor, anchor], axis=0)).astype(BF16)
                anchor = None
            if pieces:
                anchor = _fold_to_tile(
                    normalise(xnext_ref, modnext_ref, hnext_ref, pieces.pop(0), piece))
        assert not pieces and anchor is None
        out = x_ref[...] + 0.5 * mod_ref[2:3, :] * _dot(act_ref[...], w2_ref[...])
        if final_norm:
            out = _rmsnorm(out, gf_ref[...])
        o_ref[...] = out

    @pl.when(t % 2 == 0)
    def _even_tile():
        tile_step(h_even_ref, h_odd_ref)

    @pl.when(t % 2 == 1)
    def _odd_tile():
        tile_step(h_odd_ref, h_even_ref)


def _ffn(x, mod, g, which, w1, w3, w2, rows, g_final=None):
    b, s, d = x.shape
    d_ff = w1.shape[-1]
    assert d_ff % MXU_COLS == 0 and rows % (NORM_PIECES * SUBLANES) == 0
    final_norm = g_final is not None
    tiles_per_seq = s // rows
    n_tiles = b * tiles_per_seq
    following = lambda t: jnp.minimum(t + 1, n_tiles - 1)
    resident = lambda shape: pl.BlockSpec(shape, lambda t: (0, 0), pipeline_mode=pl.Buffered(1))
    weight = lambda rows_, cols_: pl.BlockSpec((None, None, rows_, cols_),
                                               lambda t: (*which, 0, 0),
                                               pipeline_mode=pl.Buffered(1))
    tile = pl.BlockSpec((None, rows, d), lambda t: (t, 0, 0))
    in_specs = [
        tile,
        pl.BlockSpec((None, rows, d), lambda t: (following(t), 0, 0)),
        pl.BlockSpec((None, 3, d), lambda t: (t // tiles_per_seq, 0, 0)),
        pl.BlockSpec((None, 3, d), lambda t: (following(t) // tiles_per_seq, 0, 0)),
        resident((1, d)),
        weight(d, d_ff),
        weight(d, d_ff),
        weight(d_ff, d),
    ]
    x_tiles = x.reshape(n_tiles, rows, d)
    args = [x_tiles, x_tiles, mod, mod, g.reshape(1, d), w1, w3, w2]
    if final_norm:
        in_specs.append(resident((1, d)))
        args.append(g_final.reshape(1, d))
    out = pl.pallas_call(
        functools.partial(_ffn_kernel, final_norm=final_norm),
        grid=(n_tiles,),
        in_specs=in_specs,
        out_specs=tile,
        out_shape=jax.ShapeDtypeStruct(x_tiles.shape, x.dtype),
        scratch_shapes=[pltpu.VMEM((rows, d), BF16), pltpu.VMEM((rows, d), BF16),
                        pltpu.VMEM((rows, d_ff), BF16)],
        compiler_params=pltpu.CompilerParams(
            dimension_semantics=("arbitrary",), vmem_limit_bytes=VMEM_LIMIT_BYTES),
        name="ffn_final" if final_norm else "ffn",
    )(*args)
    return out.reshape(x.shape)


Z_RQ = 0
Z_RK = D_RET
Z_RG = 2 * D_RET
Z_MO = 3 * D_RET
Z_WIDTH = 3 * D_RET + D_MLSTM


def _head_norm(h, g):
    mu = jnp.mean(h, axis=-1, keepdims=True)
    d = h - mu
    var = jnp.mean(d * d, axis=-1, keepdims=True)
    return d * lax.rsqrt(var + EPS) * g


def _prefix_scan_rows(t, row_id, combine, identity):
    shift = 1
    while shift < CHUNK:
        t = combine(t, jnp.where(row_id >= shift, pltpu.roll(t, shift, 0), identity))
        shift *= 2
    return t


def _stage_retention(r, hh, z_ref, cos, sin, wq_ref, wk_ref, rq_ref, rqw_ref, rk_ref, rkwt_ref):
    r0, c0 = r * CHUNK, hh * HEAD_DIM
    q = z_ref[r0:r0 + CHUNK, Z_RQ + c0:Z_RQ + c0 + HEAD_DIM]
    k = z_ref[r0:r0 + CHUNK, Z_RK + c0:Z_RK + c0 + HEAD_DIM]
    half = HEAD_DIM // 2
    qr = q * cos + pltpu.roll(q, half, 1) * sin
    kr = (k * cos + pltpu.roll(k, half, 1) * sin) * (HEAD_DIM ** -0.5)
    rq_ref[r0:r0 + CHUNK, c0:c0 + HEAD_DIM] = qr.astype(BF16)
    rqw_ref[r0:r0 + CHUNK, c0:c0 + HEAD_DIM] = (qr * wq_ref[hh]).astype(BF16)
    rk_ref[r0:r0 + CHUNK, c0:c0 + HEAD_DIM] = kr.astype(BF16)
    rkwt_ref[r, hh] = (kr * wk_ref[hh]).T.astype(BF16)


def _stage_mlstm_qk(r, blk, qkraw_ref, convw_ref, convb_ref, mq_ref, mk_ref, mkt_ref):
    r0, c0 = r * CHUNK, blk * HEAD_DIM
    acc = convb_ref[:, c0:c0 + HEAD_DIM]
    for j in range(CONV_WIDTH):
        start = SUBLANES - (CONV_WIDTH - 1) + j + r0
        acc = acc + (convw_ref[j:j + 1, c0:c0 + HEAD_DIM]
                     * qkraw_ref[start:start + CHUNK, c0:c0 + HEAD_DIM])
    qk = jax.nn.silu(acc)
    if blk < MLSTM_HEADS:
        mq_ref[r0:r0 + CHUNK, c0:c0 + HEAD_DIM] = (qk * (HEAD_DIM ** -0.5)).astype(BF16)
    else:
        hh = blk - MLSTM_HEADS
        mk_ref[r0:r0 + CHUNK, hh * HEAD_DIM:(hh + 1) * HEAD_DIM] = qk.astype(BF16)
        mkt_ref[r, hh] = qk.T


def _stage_gates(r, gate_ref, row_id, cmb_ref, bb_ref, arow_ref):
    r0 = r * CHUNK
    gates = gate_ref[r0:r0 + CHUNK, :]
    bcum = _prefix_scan_rows(gates, row_id, jnp.add, 0.0)
    a = gates - pltpu.roll(bcum, LANES - MLSTM_HEADS, 1)
    cmax = _prefix_scan_rows(a, row_id, jnp.maximum, -jnp.inf)
    arow_ref[r] = a.T[0:SUBLANES, :]
    for hh in range(MLSTM_HEADS):
        cols = slice(hh * HEAD_DIM, (hh + 1) * HEAD_DIM)
        cmb_ref[r0:r0 + CHUNK, cols] = jnp.broadcast_to(cmax[:, hh:hh + 1], (CHUNK, HEAD_DIM))
        fcol = MLSTM_HEADS + hh
        bb_ref[r0:r0 + CHUNK, cols] = jnp.broadcast_to(bcum[:, fcol:fcol + 1], (CHUNK, HEAD_DIM))


def _retention_chunk(hh, r, rq_ref, rqw_ref, rk_ref, rv_ref, rkwt_ref, decay_ref, cdec_ref,
                     state_ref, pre_ref):
    rows = slice(r * CHUNK, (r + 1) * CHUNK)
    cols = slice(hh * HEAD_DIM, (hh + 1) * HEAD_DIM)
    scores = lax.dot_general(rq_ref[rows, cols], rk_ref[rows, cols], NT_DIMS,
                             preferred_element_type=F32)
    yield
    vb = rv_ref[rows, cols]
    state = state_ref[hh]
    intra = _dot((scores * decay_ref[hh]).astype(BF16), vb)
    inter = _dot(rqw_ref[rows, cols], state.astype(BF16))
    update = _dot(rkwt_ref[r, hh], vb)
    yield
    pre_ref[rows, cols] = intra + inter
    state_ref[hh] = state * cdec_ref[hh] + update


def _mlstm_chunk(hh, r, causal, mq_ref, mk_ref, mv_ref, mkt_ref, cmb_ref, bb_ref, arow_ref,
                 c_ref, m_ref, pre_ref):
    rows = slice(r * CHUNK, (r + 1) * CHUNK)
    c0 = hh * HEAD_DIM
    cols = slice(c0, c0 + HEAD_DIM)
    qb = mq_ref[rows, cols]
    scores = lax.dot_general(qb, mk_ref[rows, cols], NT_DIMS, preferred_element_type=F32)
    yield
    v_aug = mv_ref[rows, 2 * c0:2 * c0 + 2 * HEAD_DIM]
    cmax = cmb_ref[rows, cols]
    b_l = bb_ref[rows, cols]
    a_row = arow_ref[r][hh:hh + 1, :]
    m_prev = m_ref[hh]
    state = c_ref[hh]
    mx = jnp.maximum(cmax, m_prev)
    w_d = jnp.exp(jnp.where(causal, a_row - mx, -jnp.inf))
    intra = _dot((scores * w_d).astype(BF16), v_aug)
    inter = _dot(qb, state.astype(BF16))
    b_last = b_l[CHUNK - 1:CHUNK, :]
    m_new = jnp.maximum(b_last + m_prev, b_last + cmax[CHUNK - 1:CHUNK, :])
    w_g = jnp.exp(b_last + a_row - m_new)
    update = _dot((mkt_ref[r, hh] * w_g).astype(BF16), v_aug)
    yield
    w_inter = jnp.exp(m_prev - mx)
    tot = intra + jnp.concatenate([w_inter, w_inter], axis=1) * inter
    num, den = tot[:, :HEAD_DIM], tot[:, HEAD_DIM:]
    h = num / jnp.maximum(jnp.abs(den), jnp.exp(-(b_l + mx)))
    pre_ref[rows, D_RET + c0:D_RET + c0 + HEAD_DIM] = h
    w_c = jnp.exp(b_last + m_prev - m_new)
    c_ref[hh] = jnp.concatenate([w_c, w_c], axis=1) * state + update
    m_ref[hh] = m_new


def _run_staggered(tasks, stages, fillers):
    n = len(tasks)
    n_steps = n + stages - 1
    for step in range(n_steps):
        for run_filler in fillers.get(step, ()):
            run_filler()
        for s in range(stages):
            j = step - s
            if 0 <= j < n:
                next(tasks[j], None)
    assert all(step < n_steps for step in fillers)


def _mixer_kernel(x_ref, mod_ref, g_ref, win_ref, wgate_ref, gbias_ref, cos_ref, sin_ref, convw_ref,
                  convb_ref,
                  decay_ref, wq_ref, wk_ref, cdec_ref, gret_ref, gml_ref, wout_ref,
                  o_ref,
                  h_ref, z_ref, qkraw_ref, gate_ref, pre_ref, y_ref,
                  rq_ref, rqw_ref, rk_ref, rv_ref, rkwt_ref, mq_ref, mk_ref, mv_ref, mkt_ref,
                  cmb_ref, bb_ref, arow_ref, sret_ref, c_ref, m_ref):
    rows_total = x_ref.shape[0]
    n_chunks = rows_total // CHUNK

    @pl.when(pl.program_id(1) == 0)
    def _start_of_sequence():
        sret_ref[...] = jnp.zeros_like(sret_ref)
        c_ref[...] = jnp.zeros_like(c_ref)
        m_ref[...] = jnp.zeros_like(m_ref)
        qkraw_ref[0:SUBLANES, :] = jnp.zeros((SUBLANES, qkraw_ref.shape[1]), F32)
        for hh in range(MLSTM_HEADS):
            ones_cols = slice((2 * hh + 1) * HEAD_DIM, (2 * hh + 2) * HEAD_DIM)
            mv_ref[:, ones_cols] = jnp.ones((rows_total, HEAD_DIM), BF16)

    mod = mod_ref[...]
    h_ref[...] = _modulated_norm(x_ref[...], mod, g_ref[...]).astype(BF16)

    def project(c0, width=D_RET):
        return _dot(h_ref[...], win_ref[:, c0:c0 + width])

    row_id = lax.broadcasted_iota(jnp.int32, (CHUNK, CHUNK), 0)
    col_id = lax.broadcasted_iota(jnp.int32, (CHUNK, CHUNK), 1)
    causal = row_id >= col_id

    def project_to_z(col, zcol, width=D_RET):
        z_ref[:, zcol:zcol + width] = project(col, width)

    def project_qk(c0):
        qkraw_ref[SUBLANES:SUBLANES + rows_total, c0:c0 + D_RET] = project(COL_MQK + c0)

    def project_gates():
        zg = _dot(h_ref[...], wgate_ref[...]) + gbias_ref[...]
        lane = lax.broadcasted_iota(jnp.int32, zg.shape, 1)
        gate_ref[...] = jnp.where(lane < MLSTM_HEADS, zg, _log_sigmoid(zg))

    def project_rv():
        rv_ref[...] = project(COL_RV).astype(BF16)

    def project_mv():
        mv = project(COL_MV)
        for hh in range(MLSTM_HEADS):
            mv_ref[:, 2 * hh * HEAD_DIM:(2 * hh + 1) * HEAD_DIM] = (
                mv[:, hh * HEAD_DIM:(hh + 1) * HEAD_DIM].astype(BF16))

    def stage_retention(r):
        cos = cos_ref[r * CHUNK:(r + 1) * CHUNK, :]
        sin = sin_ref[r * CHUNK:(r + 1) * CHUNK, :]
        for hh in range(RET_HEADS):
            _stage_retention(r, hh, z_ref, cos, sin, wq_ref, wk_ref, rq_ref, rqw_ref, rk_ref,
                             rkwt_ref)

    def stage_mlstm_q(r):
        for blk in range(MLSTM_HEADS):
            _stage_mlstm_qk(r, blk, qkraw_ref, convw_ref, convb_ref, mq_ref, mk_ref, mkt_ref)

    def stage_mlstm_k(r):
        for blk in range(MLSTM_HEADS, 2 * MLSTM_HEADS):
            _stage_mlstm_qk(r, blk, qkraw_ref, convw_ref, convb_ref, mq_ref, mk_ref, mkt_ref)

    def stage_gates(r):
        _stage_gates(r, gate_ref, row_id, cmb_ref, bb_ref, arow_ref)

    chunks = range(n_chunks)
    half = (n_chunks + 1) // 2
    emission = [
        ([functools.partial(project_qk, 0)], []),
        ([functools.partial(project_qk, D_RET)], [(stage_mlstm_q, r) for r in chunks[:half]]),
        ([project_gates, functools.partial(project_to_z, COL_RQ, Z_RQ)],
         [(stage_mlstm_q, r) for r in chunks[half:]]),
        ([functools.partial(project_to_z, COL_RK, Z_RK)],
         [(stage_mlstm_k, r) for r in chunks[:half]]),
        ([project_rv], [(stage_mlstm_k, r) for r in chunks[half:]] + [(stage_gates, r) for r in chunks]),
        ([project_mv], [(stage_retention, 0)]),
    ]
    for run_projections, run_stages in emission:
        for run_projection in run_projections:
            run_projection()
        for stage, r in run_stages:
            stage(r)
    qkraw_ref[0:SUBLANES, :] = qkraw_ref[rows_total:rows_total + SUBLANES, :]

    tasks = []
    for r in range(n_chunks):
        for hh in range(max(RET_HEADS, MLSTM_HEADS)):
            if hh < MLSTM_HEADS:
                tasks.append(_mlstm_chunk(hh, r, causal, mq_ref, mk_ref, mv_ref, mkt_ref,
                                          cmb_ref, bb_ref, arow_ref, c_ref, m_ref, pre_ref))
            if hh < RET_HEADS:
                tasks.append(_retention_chunk(hh, r, rq_ref, rqw_ref, rk_ref, rv_ref, rkwt_ref,
                                              decay_ref, cdec_ref, sret_ref, pre_ref))
    n_stages = 3
    n_steps = len(tasks) + n_stages - 1
    tasks_per_chunk = len(tasks) // n_chunks
    fillers = {}
    for r in range(1, n_chunks):
        fillers.setdefault((r - 1) * tasks_per_chunk, []).append(
            functools.partial(stage_retention, r))
    late = [(col + c0, zcol + c0) for col, zcol in ((COL_RG, Z_RG), (COL_MO, Z_MO))
            for c0 in range(0, D_RET, MXU_COLS)]
    for k, (col, zcol) in enumerate(late):
        fillers.setdefault(1 + k * (n_steps // len(late)), []).append(
            functools.partial(project_to_z, col, zcol, MXU_COLS))
    _run_staggered(tasks, n_stages, fillers)

    for hh in range(RET_HEADS + MLSTM_HEADS):
        is_retention = hh < RET_HEADS
        c0 = (hh if is_retention else hh - RET_HEADS) * HEAD_DIM
        pre = pre_ref[:, hh * HEAD_DIM:(hh + 1) * HEAD_DIM]
        if is_retention:
            hn = (jax.nn.silu(z_ref[:, Z_RG + c0:Z_RG + c0 + HEAD_DIM])
                  * _head_norm(pre, gret_ref[:, c0:c0 + HEAD_DIM]))
        else:
            gate = jax.nn.sigmoid(z_ref[:, Z_MO + c0:Z_MO + c0 + HEAD_DIM])
            hn = _head_norm(gate * pre, gml_ref[:, c0:c0 + HEAD_DIM])
        y_ref[:, hh * HEAD_DIM:(hh + 1) * HEAD_DIM] = hn.astype(y_ref.dtype)
    o_ref[...] = x_ref[...] + mod[2:3] * _dot(y_ref[...], wout_ref[...])


def _retention_constants():
    heads = jnp.arange(RET_HEADS, dtype=F32)
    log_gamma = jnp.log(1.0 - 2.0 ** (-5.0 - heads))
    idx = jnp.arange(CHUNK)
    diff = (idx[:, None] - idx[None, :]).astype(F32)
    decay = jnp.where(diff >= 0, jnp.exp(log_gamma[:, None, None] * jnp.maximum(diff, 0.0)), 0.0)
    w_k = jnp.exp(log_gamma[:, None] * (CHUNK - 1 - idx).astype(F32))
    w_q = jnp.exp(log_gamma[:, None] * (idx + 1).astype(F32))
    chunk_decay = jnp.exp(log_gamma * CHUNK)
    bcast = lambda t: jnp.broadcast_to(t[:, :, None], (RET_HEADS, CHUNK, HEAD_DIM))
    cdec = jnp.broadcast_to(chunk_decay[:, None, None], (RET_HEADS, 1, HEAD_DIM))
    return decay, bcast(w_q), bcast(w_k), cdec


def _gate_weights(w_in):
    n_gates = w_in.shape[-1] - COL_GATES
    return jnp.pad(w_in[:, :, COL_GATES:], ((0, 0), (0, 0), (0, LANES - n_gates))).astype(BF16)


def _mixer(x, mod, g, layer, w_main, w_gate, conv_w, conv_b, b_igate, b_fgate, g_ret, g_ml, w_out,
           cos, sin, rows):
    b, s, d = x.shape
    d_mix = D_RET + D_MLSTM
    n_gates = 2 * MLSTM_HEADS
    n_chunks = rows // CHUNK
    assert w_main.shape[1:] == (d, COL_GATES) and w_out.shape[1:] == (d_mix, d)
    gbias = jnp.pad(jnp.concatenate([b_igate, b_fgate]), (0, LANES - n_gates)).reshape(1, LANES)
    decay, w_q, w_k, cdec = _retention_constants()

    def resident(shape):
        zeros = (0,) * len(shape)
        return pl.BlockSpec(shape, lambda i, j: zeros, pipeline_mode=pl.Buffered(1))

    def layer_weight(rows_, cols_):
        return pl.BlockSpec((None, rows_, cols_), lambda i, j: (layer, 0, 0),
                            pipeline_mode=pl.Buffered(1))

    tile = lambda width: pl.BlockSpec((None, rows, width), lambda i, j: (i, j, 0))
    in_specs = [
        tile(d),
        pl.BlockSpec((None, 3, d), lambda i, j: (i, 0, 0)),
        resident((1, d)),
        layer_weight(d, COL_GATES),
        layer_weight(d, LANES),
        resident((1, LANES)),
        tile(HEAD_DIM),
        tile(HEAD_DIM),
        resident((CONV_WIDTH, 2 * D_MLSTM)),
        resident((1, 2 * D_MLSTM)),
        resident((RET_HEADS, CHUNK, CHUNK)),
        resident((RET_HEADS, CHUNK, HEAD_DIM)),
        resident((RET_HEADS, CHUNK, HEAD_DIM)),
        resident((RET_HEADS, 1, HEAD_DIM)),
        resident((1, D_RET)),
        resident((1, D_MLSTM)),
        layer_weight(d_mix, d),
    ]
    head_tiles = lambda heads, dtype: pltpu.VMEM((n_chunks, heads, HEAD_DIM, CHUNK), dtype)
    scratch_shapes = [
        pltpu.VMEM((rows, d), BF16),
        pltpu.VMEM((rows, Z_WIDTH), F32),
        pltpu.VMEM((rows + SUBLANES, 2 * D_MLSTM), F32),
        pltpu.VMEM((rows, LANES), F32),
        pltpu.VMEM((rows, d_mix), F32),
        pltpu.VMEM((rows, d_mix), BF16),
        pltpu.VMEM((rows, D_RET), BF16),
        pltpu.VMEM((rows, D_RET), BF16),
        pltpu.VMEM((rows, D_RET), BF16),
        pltpu.VMEM((rows, D_RET), BF16),
        head_tiles(RET_HEADS, BF16),
        pltpu.VMEM((rows, D_MLSTM), BF16),
        pltpu.VMEM((rows, D_MLSTM), BF16),
        pltpu.VMEM((rows, 2 * D_MLSTM), BF16),
        head_tiles(MLSTM_HEADS, F32),
        pltpu.VMEM((rows, D_MLSTM), F32),
        pltpu.VMEM((rows, D_MLSTM), F32),
        pltpu.VMEM((n_chunks, SUBLANES, CHUNK), F32),
        pltpu.VMEM((RET_HEADS, HEAD_DIM, HEAD_DIM), F32),
        pltpu.VMEM((MLSTM_HEADS, HEAD_DIM, 2 * HEAD_DIM), F32),
        pltpu.VMEM((MLSTM_HEADS, 1, LANES), F32),
    ]
    return pl.pallas_call(
        _mixer_kernel,
        grid=(b, s // rows),
        in_specs=in_specs,
        out_specs=tile(d),
        out_shape=jax.ShapeDtypeStruct(x.shape, x.dtype),
        scratch_shapes=scratch_shapes,
        compiler_params=pltpu.CompilerParams(
            dimension_semantics=("arbitrary", "arbitrary"), vmem_limit_bytes=VMEM_LIMIT_BYTES),
        name="mixer",
    )(x, mod, g.reshape(1, d), w_main, w_gate, gbias, cos, sin, conv_w, conv_b.reshape(1, -1),
      decay, w_q, w_k, cdec, g_ret.reshape(1, -1), g_ml.reshape(1, -1), w_out)


@jax.jit
def kernel(x, c, positions, norm_g, w_ada, b_ada, w_ff1, w_ff3, w_ff2, w_in, conv_w, conv_b,
           b_igate, b_fgate, g_ret_norm, g_mlstm_norm, w_out, g_final):
    depth = w_in.shape[0]
    ffn_rows, mix_rows, prepare_rows = _tiles(x.shape[1])
    mods = _ada_mod(c, w_ada, b_ada)
    d, d_ff = w_ff1.shape[-2:]
    cos, sin, (*ffn_weights, w_main, w_out) = _prepare(
        positions, prepare_rows,
        [(w_ff1, d_ff), (w_ff3, d_ff), (w_ff2, d), (w_in, COL_GATES), (w_out, d)])
    w_gate = _gate_weights(w_in)
    for l in range(depth):
        x = _ffn(x, mods[l, 0], norm_g[l, 0], (l, 0), *ffn_weights, ffn_rows)
        x = _mixer(x, mods[l, 1], norm_g[l, 1], l, w_main, w_gate, conv_w[l], conv_b[l],
                   b_igate[l], b_fgate[l], g_ret_norm[l], g_mlstm_norm[l], w_out, cos, sin,
                   mix_rows)
        x = _ffn(x, mods[l, 2], norm_g[l, 2], (l, 1), *ffn_weights, ffn_rows,
                 g_final=g_final if l == depth - 1 else None)
    return x
```

```python
import functools

import jax
import jax.numpy as jnp
from jax import lax
from jax.experimental import pallas as pl
from jax.experimental.pallas import tpu as pltpu

F32 = jnp.float32
BF16 = jnp.bfloat16

RET_HEADS = 4
MLSTM_HEADS = 4
HEAD_DIM = 128
CHUNK = 128
CONV_WIDTH = 4
ROPE_BASE = 10000.0
EPS = 1e-6
N_SUBLAYERS = 3
D_RET = RET_HEADS * HEAD_DIM
D_MLSTM = MLSTM_HEADS * HEAD_DIM

LANES = 128
SUBLANES = 8
MXU_COLS = 256
VMEM_LIMIT_BYTES = 56 * 1024 * 1024

COL_RQ = 0
COL_RK = D_RET
COL_RV = 2 * D_RET
COL_RG = 3 * D_RET
COL_MQK = 4 * D_RET
COL_MV = COL_MQK + 2 * D_MLSTM
COL_MO = COL_MV + D_MLSTM
COL_GATES = COL_MO + D_MLSTM

NT_DIMS = (((1,), (1,)), ((), ()))


def _tiles(seq):
    ffn_rows = min(1024, seq)
    mix_rows = min(512, seq)
    prepare_rows = seq // 2
    assert seq % ffn_rows == 0 and seq % mix_rows == 0 and mix_rows % CHUNK == 0
    return ffn_rows, mix_rows, prepare_rows


def _dot(a, b):
    return jnp.dot(a, b, preferred_element_type=F32)


def _rmsnorm(x, g):
    return x * lax.rsqrt(jnp.mean(x * x, axis=-1, keepdims=True) + EPS) * g


def _modulated_norm(x, mod, g):
    shift, scale = mod[0:1], mod[1:2]
    return _rmsnorm(x, g) * (1.0 + scale) + shift


def _log_sigmoid(x):
    return jnp.minimum(x, 0.0) - jnp.log1p(jnp.exp(-jnp.abs(x)))


def _fold_to_tile(t):
    acc = None
    for r0 in range(0, t.shape[0], SUBLANES):
        for c0 in range(0, t.shape[1], LANES):
            blk = t[r0:r0 + SUBLANES, c0:c0 + LANES]
            acc = blk if acc is None else acc + blk
    return acc


def _ada_kernel(c_ref, w_ref, b_ref, o_ref):
    sc = jax.nn.silu(c_ref[...])
    w = w_ref[...]
    s_hi = sc.astype(BF16)
    s_lo = (sc - s_hi.astype(F32)).astype(BF16)
    w_hi = w.astype(BF16)
    w_lo = (w - w_hi.astype(F32)).astype(BF16)
    n = sc.shape[0]
    both = _dot(jnp.concatenate([s_hi, s_lo], axis=0), w_hi)
    o_ref[...] = both[:n] + both[n:] + _dot(s_hi, w_lo) + b_ref[...]


def _ada_mod(c, w_ada, b_ada):
    depth, nsub, d, d3 = w_ada.shape
    n = depth * nsub
    b = c.shape[0]
    out = pl.pallas_call(
        _ada_kernel,
        grid=(n, d3 // d),
        in_specs=[
            pl.BlockSpec((b, d), lambda i, j: (0, 0)),
            pl.BlockSpec((None, d, d), lambda i, j: (i, 0, j)),
            pl.BlockSpec((None, 1, d), lambda i, j: (i, 0, j)),
        ],
        out_specs=pl.BlockSpec((None, b, d), lambda i, j: (i, 0, j)),
        out_shape=jax.ShapeDtypeStruct((n, b, d3), F32),
        compiler_params=pltpu.CompilerParams(
            dimension_semantics=("arbitrary", "arbitrary"), vmem_limit_bytes=VMEM_LIMIT_BYTES),
        name="ada_mod",
    )(c, w_ada.reshape(n, d, d3), b_ada.reshape(n, 1, d3))
    return out.reshape(depth, nsub, b, d3 // d, d)


def _prepare_kernel(pos_ref, invf_ref, *refs):
    n_weights = (len(refs) - 2) // 2
    weight_refs = refs[:n_weights]
    cos_ref, sin_ref = refs[n_weights:n_weights + 2]
    for src_ref, dst_ref in zip(weight_refs, refs[n_weights + 2:]):
        dst_ref[...] = src_ref[...].astype(BF16)
    rows, half = pos_ref.shape[0], HEAD_DIM // 2
    pos = pos_ref[...].astype(F32)
    low = lax.broadcasted_iota(jnp.int32, (rows // 2, HEAD_DIM), 1) < half
    ang = jnp.where(low, pos[:rows // 2], pos[rows // 2:]) * invf_ref[...]
    c, s = jnp.cos(ang), jnp.sin(ang)
    c_swapped, s_swapped = pltpu.roll(c, half, 1), pltpu.roll(s, half, 1)
    cos_ref[:rows // 2, :] = jnp.where(low, c, c_swapped)
    cos_ref[rows // 2:, :] = jnp.where(low, c_swapped, c)
    sin_ref[:rows // 2, :] = jnp.where(low, -s, s_swapped)
    sin_ref[rows // 2:, :] = jnp.where(low, -s_swapped, s)


def _prepare(positions, rows, weights):
    b, s = positions.shape
    assert rows % (2 * SUBLANES) == 0 and s % rows == 0
    n_j = s // rows
    n_steps = b * n_j
    inv_freq = ROPE_BASE ** (-jnp.arange(0, HEAD_DIM, 2, dtype=F32) / HEAD_DIM)
    invf = jnp.concatenate([inv_freq, inv_freq]).reshape(1, HEAD_DIM)
    table = jax.ShapeDtypeStruct((b, s, HEAD_DIM), F32)
    const = pl.BlockSpec((1, HEAD_DIM), lambda i, j: (0, 0))
    tile = pl.BlockSpec((None, rows, HEAD_DIM), lambda i, j: (i, j, 0))
    flat, weight_specs, cast_shapes = [], [], []
    for w, n_cols in weights:
        w2d = w.reshape(-1, w.shape[-1])
        block_rows = w2d.shape[0] // n_steps
        packed = 2 * SUBLANES
        assert block_rows * n_steps == w2d.shape[0] and block_rows % packed == 0
        assert n_cols % LANES == 0
        flat.append(w2d)
        weight_specs.append(pl.BlockSpec((block_rows, n_cols), lambda i, j: (i * n_j + j, 0)))
        cast_shapes.append(jax.ShapeDtypeStruct((w2d.shape[0], n_cols), BF16))
    outs = pl.pallas_call(
        _prepare_kernel,
        grid=(b, n_j),
        in_specs=[pl.BlockSpec((None, rows, 1), lambda i, j: (i, j, 0)), const] + weight_specs,
        out_specs=[tile, tile] + weight_specs,
        out_shape=[table, table] + cast_shapes,
        compiler_params=pltpu.CompilerParams(
            dimension_semantics=("arbitrary", "arbitrary"), vmem_limit_bytes=VMEM_LIMIT_BYTES),
        name="prepare",
    )(positions.reshape(b, s, 1), invf, *flat)
    casts = [o.reshape(*w.shape[:-1], n_cols) for o, (w, n_cols) in zip(outs[2:], weights)]
    return outs[0], outs[1], casts


NORM_PIECES = 8


def _ffn_kernel(x_ref, xnext_ref, mod_ref, modnext_ref, g_ref, w1_ref, w3_ref, w2_ref, *rest,
                final_norm):
    if final_norm:
        gf_ref, o_ref, h_even_ref, h_odd_ref, act_ref = rest
    else:
        o_ref, h_even_ref, h_odd_ref, act_ref = rest
    t = pl.program_id(0)
    rows = x_ref.shape[0]
    d_ff = w1_ref.shape[1]

    def normalise(src_ref, src_mod_ref, dst_ref, r0, n):
        hn = _modulated_norm(src_ref[r0:r0 + n, :], src_mod_ref[...], g_ref[...])
        dst_ref[r0:r0 + n, :] = hn.astype(BF16)
        return hn

    @pl.when(t == 0)
    def _first_tile():
        normalise(x_ref, mod_ref, h_even_ref, 0, rows)

    def tile_step(h_ref, hnext_ref):
        piece = rows // NORM_PIECES
        pieces = list(range(0, rows, piece))
        anchor = None
        for c0 in range(0, d_ff, MXU_COLS):
            h = h_ref[...]
            a = _dot(h, w1_ref[:, c0:c0 + MXU_COLS])
            b = _dot(h, w3_ref[:, c0:c0 + MXU_COLS])
            act = jax.nn.silu(a) * b
            act_ref[:, c0:c0 + MXU_COLS] = act.astype(BF16)
            if anchor is not None:
                packed = 2 * SUBLANES
                act_ref[0:packed, c0:c0 + LANES] = (
                    act[0:packed, 0:LANES]
                    + 0.0 * jnp.concatenate([anchor, anchor], axis=0)).astype(BF16)
                anchor = None
            if pieces:
                anchor = _fold_to_tile(
                    normalise(xnext_ref, modnext_ref, hnext_ref, pieces.pop(0), piece))
        assert not pieces and anchor is None
        out = x_ref[...] + 0.5 * mod_ref[2:3, :] * _dot(act_ref[...], w2_ref[...])
        if final_norm:
            out = _rmsnorm(out, gf_ref[...])
        o_ref[...] = out

    @pl.when(t % 2 == 0)
    def _even_tile():
        tile_step(h_even_ref, h_odd_ref)

    @pl.when(t % 2 == 1)
    def _odd_tile():
        tile_step(h_odd_ref, h_even_ref)


def _ffn(x, mod, g, which, w1, w3, w2, rows, g_final=None):
    b, s, d = x.shape
    d_ff = w1.shape[-1]
    assert d_ff % MXU_COLS == 0 and rows % (NORM_PIECES * SUBLANES) == 0
    final_norm = g_final is not None
    tiles_per_seq = s // rows
    n_tiles = b * tiles_per_seq
    following = lambda t: jnp.minimum(t + 1, n_tiles - 1)
    resident = lambda shape: pl.BlockSpec(shape, lambda t: (0, 0), pipeline_mode=pl.Buffered(1))
    weight = lambda rows_, cols_: pl.BlockSpec((None, None, rows_, cols_),
                                               lambda t: (*which, 0, 0),
                                               pipeline_mode=pl.Buffered(1))
    tile = pl.BlockSpec((None, rows, d), lambda t: (t, 0, 0))
    in_specs = [
        tile,
        pl.BlockSpec((None, rows, d), lambda t: (following(t), 0, 0)),
        pl.BlockSpec((None, 3, d), lambda t: (t // tiles_per_seq, 0, 0)),
        pl.BlockSpec((None, 3, d), lambda t: (following(t) // tiles_per_seq, 0, 0)),
        resident((1, d)),
        weight(d, d_ff),
        weight(d, d_ff),
        weight(d_ff, d),
    ]
    x_tiles = x.reshape(n_tiles, rows, d)
    args = [x_tiles, x_tiles, mod, mod, g.reshape(1, d), w1, w3, w2]
    if final_norm:
        in_specs.append(resident((1, d)))
        args.append(g_final.reshape(1, d))
    out = pl.pallas_call(
        functools.partial(_ffn_kernel, final_norm=final_norm),
        grid=(n_tiles,),
        in_specs=in_specs,
        out_specs=tile,
        out_shape=jax.ShapeDtypeStruct(x_tiles.shape, x.dtype),
        scratch_shapes=[pltpu.VMEM((rows, d), BF16), pltpu.VMEM((rows, d), BF16),
                        pltpu.VMEM((rows, d_ff), BF16)],
        compiler_params=pltpu.CompilerParams(
            dimension_semantics=("arbitrary",), vmem_limit_bytes=VMEM_LIMIT_BYTES),
        name="ffn_final" if final_norm else "ffn",
    )(*args)
    return out.reshape(x.shape)


Z_RQ = 0
Z_RK = D_RET
Z_RG = 2 * D_RET
Z_MO = 3 * D_RET
Z_WIDTH = 3 * D_RET + D_MLSTM


def _head_norm(h, g):
    mu = jnp.mean(h, axis=-1, keepdims=True)
    d = h - mu
    var = jnp.mean(d * d, axis=-1, keepdims=True)
    return d * lax.rsqrt(var + EPS) * g


def _prefix_scan_rows(t, row_id, combine, identity):
    shift = 1
    while shift < CHUNK:
        t = combine(t, jnp.where(row_id >= shift, pltpu.roll(t, shift, 0), identity))
        shift *= 2
    return t


def _stage_retention(r, hh, z_ref, cos, sin, wq_ref, wk_ref, rq_ref, rqw_ref, rk_ref, rkwt_ref):
    r0, c0 = r * CHUNK, hh * HEAD_DIM
    q = z_ref[r0:r0 + CHUNK, Z_RQ + c0:Z_RQ + c0 + HEAD_DIM]
    k = z_ref[r0:r0 + CHUNK, Z_RK + c0:Z_RK + c0 + HEAD_DIM]
    half = HEAD_DIM // 2
    qr = q * cos + pltpu.roll(q, half, 1) * sin
    kr = (k * cos + pltpu.roll(k, half, 1) * sin) * (HEAD_DIM ** -0.5)
    rq_ref[r0:r0 + CHUNK, c0:c0 + HEAD_DIM] = qr.astype(BF16)
    rqw_ref[r0:r0 + CHUNK, c0:c0 + HEAD_DIM] = (qr * wq_ref[hh]).astype(BF16)
    rk_ref[r0:r0 + CHUNK, c0:c0 + HEAD_DIM] = kr.astype(BF16)
    rkwt_ref[r, hh] = (kr * wk_ref[hh]).T.astype(BF16)


def _stage_mlstm_qk(r, blk, qkraw_ref, convw_ref, convb_ref, mq_ref, mk_ref, mkt_ref):
    r0, c0 = r * CHUNK, blk * HEAD_DIM
    acc = convb_ref[:, c0:c0 + HEAD_DIM]
    for j in range(CONV_WIDTH):
        start = SUBLANES - (CONV_WIDTH - 1) + j + r0
        acc = acc + (convw_ref[j:j + 1, c0:c0 + HEAD_DIM]
                     * qkraw_ref[start:start + CHUNK, c0:c0 + HEAD_DIM])
    qk = jax.nn.silu(acc)
    if blk < MLSTM_HEADS:
        mq_ref[r0:r0 + CHUNK, c0:c0 + HEAD_DIM] = (qk * (HEAD_DIM ** -0.5)).astype(BF16)
    else:
        hh = blk - MLSTM_HEADS
        mk_ref[r0:r0 + CHUNK, hh * HEAD_DIM:(hh + 1) * HEAD_DIM] = qk.astype(BF16)
        mkt_ref[r, hh] = qk.T


def _stage_gates(r, gate_ref, row_id, cmb_ref, bb_ref, arow_ref):
    r0 = r * CHUNK
    gates = gate_ref[r0:r0 + CHUNK, :]
    bcum = _prefix_scan_rows(gates, row_id, jnp.add, 0.0)
    a = gates - pltpu.roll(bcum, LANES - MLSTM_HEADS, 1)
    cmax = _prefix_scan_rows(a, row_id, jnp.maximum, -jnp.inf)
    arow_ref[r] = a.T[0:SUBLANES, :]
    for hh in range(MLSTM_HEADS):
        cols = slice(hh * HEAD_DIM, (hh + 1) * HEAD_DIM)
        cmb_ref[r0:r0 + CHUNK, cols] = jnp.broadcast_to(cmax[:, hh:hh + 1], (CHUNK, HEAD_DIM))
        fcol = MLSTM_HEADS + hh
        bb_ref[r0:r0 + CHUNK, cols] = jnp.broadcast_to(bcum[:, fcol:fcol + 1], (CHUNK, HEAD_DIM))


def _retention_chunk(hh, r, rq_ref, rqw_ref, rk_ref, rv_ref, rkwt_ref, decay_ref, cdec_ref,
                     state_ref, pre_ref):
    rows = slice(r * CHUNK, (r + 1) * CHUNK)
    cols = slice(hh * HEAD_DIM, (hh + 1) * HEAD_DIM)
    scores = lax.dot_general(rq_ref[rows, cols], rk_ref[rows, cols], NT_DIMS,
                             preferred_element_type=F32)
    yield
    vb = rv_ref[rows, cols]
    state = state_ref[hh]
    intra = _dot((scores * decay_ref[hh]).astype(BF16), vb)
    inter = _dot(rqw_ref[rows, cols], state.astype(BF16))
    update = _dot(rkwt_ref[r, hh], vb)
    yield
    pre_ref[rows, cols] = intra + inter
    state_ref[hh] = state * cdec_ref[hh] + update


def _mlstm_chunk(hh, r, causal, mq_ref, mk_ref, mv_ref, mkt_ref, cmb_ref, bb_ref, arow_ref,
                 c_ref, m_ref, pre_ref):
    rows = slice(r * CHUNK, (r + 1) * CHUNK)
    c0 = hh * HEAD_DIM
    cols = slice(c0, c0 + HEAD_DIM)
    qb = mq_ref[rows, cols]
    scores = lax.dot_general(qb, mk_ref[rows, cols], NT_DIMS, preferred_element_type=F32)
    yield
    v_aug = mv_ref[rows, 2 * c0:2 * c0 + 2 * HEAD_DIM]
    cmax = cmb_ref[rows, cols]
    b_l = bb_ref[rows, cols]
    a_row = arow_ref[r][hh:hh + 1, :]
    m_prev = m_ref[hh]
    state = c_ref[hh]
    mx = jnp.maximum(cmax, m_prev)
    w_d = jnp.exp(jnp.where(causal, a_row - mx, -jnp.inf))
    intra = _dot((scores * w_d).astype(BF16), v_aug)
    inter = _dot(qb, state.astype(BF16))
    b_last = b_l[CHUNK - 1:CHUNK, :]
    m_new = jnp.maximum(b_last + m_prev, b_last + cmax[CHUNK - 1:CHUNK, :])
    w_g = jnp.exp(b_last + a_row - m_new)
    update = _dot((mkt_ref[r, hh] * w_g).astype(BF16), v_aug)
    yield
    w_inter = jnp.exp(m_prev - mx)
    tot = intra + jnp.concatenate([w_inter, w_inter], axis=1) * inter
    num, den = tot[:, :HEAD_DIM], tot[:, HEAD_DIM:]
    h = num / jnp.maximum(jnp.abs(den), jnp.exp(-(b_l + mx)))
    pre_ref[rows, D_RET + c0:D_RET + c0 + HEAD_DIM] = h
    w_c = jnp.exp(b_last + m_prev - m_new)
    c_ref[hh] = jnp.concatenate([w_c, w_c], axis=1) * state + update
    m_ref[hh] = m_new


def _run_staggered(tasks, stages, fillers):
    n = len(tasks)
    n_steps = n + stages - 1
    for step in range(n_steps):
        for run_filler in fillers.get(step, ()):
            run_filler()
        for s in range(stages):
            j = step - s
            if 0 <= j < n:
                next(tasks[j], None)
    assert all(step < n_steps for step in fillers)


def _mixer_kernel(x_ref, mod_ref, g_ref, win_ref, wgate_ref, gbias_ref, cos_ref, sin_ref, convw_ref,
                  convb_ref,
                  decay_ref, wq_ref, wk_ref, cdec_ref, gret_ref, gml_ref, wout_ref,
                  o_ref,
                  h_ref, z_ref, qkraw_ref, gate_ref, pre_ref, y_ref,
                  rq_ref, rqw_ref, rk_ref, rv_ref, rkwt_ref, mq_ref, mk_ref, mv_ref, mkt_ref,
                  cmb_ref, bb_ref, arow_ref, sret_ref, c_ref, m_ref):
    rows_total = x_ref.shape[0]
    n_chunks = rows_total // CHUNK

    @pl.when(pl.program_id(1) == 0)
    def _start_of_sequence():
        sret_ref[...] = jnp.zeros_like(sret_ref)
        c_ref[...] = jnp.zeros_like(c_ref)
        m_ref[...] = jnp.zeros_like(m_ref)
        qkraw_ref[0:SUBLANES, :] = jnp.zeros((SUBLANES, qkraw_ref.shape[1]), F32)
        for hh in range(MLSTM_HEADS):
            ones_cols = slice((2 * hh + 1) * HEAD_DIM, (2 * hh + 2) * HEAD_DIM)
            mv_ref[:, ones_cols] = jnp.ones((rows_total, HEAD_DIM), BF16)

    mod = mod_ref[...]
    h_ref[...] = _modulated_norm(x_ref[...], mod, g_ref[...]).astype(BF16)

    def project(c0, width=D_RET):
        return _dot(h_ref[...], win_ref[:, c0:c0 + width])

    row_id = lax.broadcasted_iota(jnp.int32, (CHUNK, CHUNK), 0)
    col_id = lax.broadcasted_iota(jnp.int32, (CHUNK, CHUNK), 1)
    causal = row_id >= col_id

    def project_to_z(col, zcol, width=D_RET):
        z_ref[:, zcol:zcol + width] = project(col, width)

    def project_qk(c0):
        qkraw_ref[SUBLANES:SUBLANES + rows_total, c0:c0 + D_RET] = project(COL_MQK + c0)

    def project_gates():
        zg = _dot(h_ref[...], wgate_ref[...]) + gbias_ref[...]
        lane = lax.broadcasted_iota(jnp.int32, zg.shape, 1)
        gate_ref[...] = jnp.where(lane < MLSTM_HEADS, zg, _log_sigmoid(zg))

    def project_rv():
        rv_ref[...] = project(COL_RV).astype(BF16)

    def project_mv():
        mv = project(COL_MV)
        for hh in range(MLSTM_HEADS):
            mv_ref[:, 2 * hh * HEAD_DIM:(2 * hh + 1) * HEAD_DIM] = (
                mv[:, hh * HEAD_DIM:(hh + 1) * HEAD_DIM].astype(BF16))

    def stage_retention(r):
        cos = cos_ref[r * CHUNK:(r + 1) * CHUNK, :]
        sin = sin_ref[r * CHUNK:(r + 1) * CHUNK, :]
        for hh in range(RET_HEADS):
            _stage_retention(r, hh, z_ref, cos, sin, wq_ref, wk_ref, rq_ref, rqw_ref, rk_ref,
                             rkwt_ref)

    def stage_mlstm_q(r):
        for blk in range(MLSTM_HEADS):
            _stage_mlstm_qk(r, blk, qkraw_ref, convw_ref, convb_ref, mq_ref, mk_ref, mkt_ref)

    def stage_mlstm_k(r):
        for blk in range(MLSTM_HEADS, 2 * MLSTM_HEADS):
            _stage_mlstm_qk(r, blk, qkraw_ref, convw_ref, convb_ref, mq_ref, mk_ref, mkt_ref)

    def stage_gates(r):
        _stage_gates(r, gate_ref, row_id, cmb_ref, bb_ref, arow_ref)

    chunks = range(n_chunks)
    half = (n_chunks + 1) // 2
    emission = [
        ([functools.partial(project_qk, 0)], []),
        ([functools.partial(project_qk, D_RET)], [(stage_mlstm_q, r) for r in chunks[:half]]),
        ([project_gates, functools.partial(project_to_z, COL_RQ, Z_RQ)],
         [(stage_mlstm_q, r) for r in chunks[half:]]),
        ([functools.partial(project_to_z, COL_RK, Z_RK)],
         [(stage_mlstm_k, r) for r in chunks[:half]]),
        ([project_rv], [(stage_mlstm_k, r) for r in chunks[half:]] + [(stage_gates, r) for r in chunks]),
        ([project_mv], [(stage_retention, 0)]),
    ]
    for run_projections, run_stages in emission:
        for run_projection in run_projections:
            run_projection()
        for stage, r in run_stages:
            stage(r)
    qkraw_ref[0:SUBLANES, :] = qkraw_ref[rows_total:rows_total + SUBLANES, :]

    tasks = []
    for r in range(n_chunks):
        for hh in range(max(RET_HEADS, MLSTM_HEADS)):
            if hh < MLSTM_HEADS:
                tasks.append(_mlstm_chunk(hh, r, causal, mq_ref, mk_ref, mv_ref, mkt_ref,
                                          cmb_ref, bb_ref, arow_ref, c_ref, m_ref, pre_ref))
            if hh < RET_HEADS:
                tasks.append(_retention_chunk(hh, r, rq_ref, rqw_ref, rk_ref, rv_ref, rkwt_ref,
                                              decay_ref, cdec_ref, sret_ref, pre_ref))
    n_stages = 3
    n_steps = len(tasks) + n_stages - 1
    tasks_per_chunk = len(tasks) // n_chunks
    fillers = {}
    for r in range(1, n_chunks):
        fillers.setdefault((r - 1) * tasks_per_chunk, []).append(
            functools.partial(stage_retention, r))
    late = [(col + c0, zcol + c0) for col, zcol in ((COL_RG, Z_RG), (COL_MO, Z_MO))
            for c0 in range(0, D_RET, MXU_COLS)]
    for k, (col, zcol) in enumerate(late):
        fillers.setdefault(1 + k * (n_steps // len(late)), []).append(
            functools.partial(project_to_z, col, zcol, MXU_COLS))
    _run_staggered(tasks, n_stages, fillers)

    for hh in range(RET_HEADS + MLSTM_HEADS):
        is_retention = hh < RET_HEADS
        c0 = (hh if is_retention else hh - RET_HEADS) * HEAD_DIM
        pre = pre_ref[:, hh * HEAD_DIM:(hh + 1) * HEAD_DIM]
        if is_retention:
            hn = (jax.nn.silu(z_ref[:, Z_RG + c0:Z_RG + c0 + HEAD_DIM])
                  * _head_norm(pre, gret_ref[:, c0:c0 + HEAD_DIM]))
        else:
            gate = jax.nn.sigmoid(z_ref[:, Z_MO + c0:Z_MO + c0 + HEAD_DIM])
            hn = _head_norm(gate * pre, gml_ref[:, c0:c0 + HEAD_DIM])
        y_ref[:, hh * HEAD_DIM:(hh + 1) * HEAD_DIM] = hn.astype(y_ref.dtype)
    o_ref[...] = x_ref[...] + mod[2:3] * _dot(y_ref[...], wout_ref[...])


def _retention_constants():
    heads = jnp.arange(RET_HEADS, dtype=F32)
    log_gamma = jnp.log(1.0 - 2.0 ** (-5.0 - heads))
    idx = jnp.arange(CHUNK)
    diff = (idx[:, None] - idx[None, :]).astype(F32)
    decay = jnp.where(diff >= 0, jnp.exp(log_gamma[:, None, None] * jnp.maximum(diff, 0.0)), 0.0)
    w_k = jnp.exp(log_gamma[:, None] * (CHUNK - 1 - idx).astype(F32))
    w_q = jnp.exp(log_gamma[:, None] * (idx + 1).astype(F32))
    chunk_decay = jnp.exp(log_gamma * CHUNK)
    bcast = lambda t: jnp.broadcast_to(t[:, :, None], (RET_HEADS, CHUNK, HEAD_DIM))
    cdec = jnp.broadcast_to(chunk_decay[:, None, None], (RET_HEADS, 1, HEAD_DIM))
    return decay, bcast(w_q), bcast(w_k), cdec


def _gate_weights(w_in):
    n_gates = w_in.shape[-1] - COL_GATES
    return jnp.pad(w_in[:, :, COL_GATES:], ((0, 0), (0, 0), (0, LANES - n_gates))).astype(BF16)


def _mixer(x, mod, g, layer, w_main, w_gate, conv_w, conv_b, b_igate, b_fgate, g_ret, g_ml, w_out,
           cos, sin, rows):
    b, s, d = x.shape
    d_mix = D_RET + D_MLSTM
    n_gates = 2 * MLSTM_HEADS
    n_chunks = rows // CHUNK
    assert w_main.shape[1] == d and w_main.shape[2] >= COL_GATES and w_out.shape[1:] == (d_mix, d)
    gbias = jnp.pad(jnp.concatenate([b_igate, b_fgate]), (0, LANES - n_gates)).reshape(1, LANES)
    decay, w_q, w_k, cdec = _retention_constants()

    def resident(shape):
        zeros = (0,) * len(shape)
        return pl.BlockSpec(shape, lambda i, j: zeros, pipeline_mode=pl.Buffered(1))

    def layer_weight(rows_, cols_):
        return pl.BlockSpec((None, rows_, cols_), lambda i, j: (layer, 0, 0),
                            pipeline_mode=pl.Buffered(1))

    tile = lambda width: pl.BlockSpec((None, rows, width), lambda i, j: (i, j, 0))
    in_specs = [
        tile(d),
        pl.BlockSpec((None, 3, d), lambda i, j: (i, 0, 0)),
        resident((1, d)),
        layer_weight(d, COL_GATES),
        layer_weight(d, LANES),
        resident((1, LANES)),
        tile(HEAD_DIM),
        tile(HEAD_DIM),
        resident((CONV_WIDTH, 2 * D_MLSTM)),
        resident((1, 2 * D_MLSTM)),
        resident((RET_HEADS, CHUNK, CHUNK)),
        resident((RET_HEADS, CHUNK, HEAD_DIM)),
        resident((RET_HEADS, CHUNK, HEAD_DIM)),
        resident((RET_HEADS, 1, HEAD_DIM)),
        resident((1, D_RET)),
        resident((1, D_MLSTM)),
        layer_weight(d_mix, d),
    ]
    head_tiles = lambda heads, dtype: pltpu.VMEM((n_chunks, heads, HEAD_DIM, CHUNK), dtype)
    scratch_shapes = [
        pltpu.VMEM((rows, d), BF16),
        pltpu.VMEM((rows, Z_WIDTH), F32),
        pltpu.VMEM((rows + SUBLANES, 2 * D_MLSTM), F32),
        pltpu.VMEM((rows, LANES), F32),
        pltpu.VMEM((rows, d_mix), F32),
        pltpu.VMEM((rows, d_mix), BF16),
        pltpu.VMEM((rows, D_RET), BF16),
        pltpu.VMEM((rows, D_RET), BF16),
        pltpu.VMEM((rows, D_RET), BF16),
        pltpu.VMEM((rows, D_RET), BF16),
        head_tiles(RET_HEADS, BF16),
        pltpu.VMEM((rows, D_MLSTM), BF16),
        pltpu.VMEM((rows, D_MLSTM), BF16),
        pltpu.VMEM((rows, 2 * D_MLSTM), BF16),
        head_tiles(MLSTM_HEADS, F32),
        pltpu.VMEM((rows, D_MLSTM), F32),
        pltpu.VMEM((rows, D_MLSTM), F32),
        pltpu.VMEM((n_chunks, SUBLANES, CHUNK), F32),
        pltpu.VMEM((RET_HEADS, HEAD_DIM, HEAD_DIM), F32),
        pltpu.VMEM((MLSTM_HEADS, HEAD_DIM, 2 * HEAD_DIM), F32),
        pltpu.VMEM((MLSTM_HEADS, 1, LANES), F32),
    ]
    return pl.pallas_call(
        _mixer_kernel,
        grid=(b, s // rows),
        in_specs=in_specs,
        out_specs=tile(d),
        out_shape=jax.ShapeDtypeStruct(x.shape, x.dtype),
        scratch_shapes=scratch_shapes,
        compiler_params=pltpu.CompilerParams(
            dimension_semantics=("arbitrary", "arbitrary"), vmem_limit_bytes=VMEM_LIMIT_BYTES),
        name="mixer",
    )(x, mod, g.reshape(1, d), w_main, w_gate, gbias, cos, sin, conv_w, conv_b.reshape(1, -1),
      decay, w_q, w_k, cdec, g_ret.reshape(1, -1), g_ml.reshape(1, -1), w_out)


@jax.jit
def kernel(x, c, positions, norm_g, w_ada, b_ada, w_ff1, w_ff3, w_ff2, w_in, conv_w, conv_b,
           b_igate, b_fgate, g_ret_norm, g_mlstm_norm, w_out, g_final):
    depth = w_in.shape[0]
    ffn_rows, mix_rows, prepare_rows = _tiles(x.shape[1])
    mods = _ada_mod(c, w_ada, b_ada)
    d, d_ff = w_ff1.shape[-2:]
    cos, sin, (*ffn_weights, w_out) = _prepare(
        positions, prepare_rows, [(w_ff1, d_ff), (w_ff3, d_ff), (w_ff2, d), (w_out, d)])
    w_main = w_in.astype(BF16)
    w_gate = _gate_weights(w_main)
    for l in range(depth):
        x = _ffn(x, mods[l, 0], norm_g[l, 0], (l, 0), *ffn_weights, ffn_rows)
        x = _mixer(x, mods[l, 1], norm_g[l, 1], l, w_main, w_gate, conv_w[l], conv_b[l],
                   b_igate[l], b_fgate[l], g_ret_norm[l], g_mlstm_norm[l], w_out, cos, sin,
                   mix_rows)
        x = _ffn(x, mods[l, 2], norm_g[l, 2], (l, 1), *ffn_weights, ffn_rows,
                 g_final=g_final if l == depth - 1 else None)
    return x
```

```python
import functools

import jax
import jax.numpy as jnp
from jax import lax
from jax.experimental import pallas as pl
from jax.experimental.pallas import tpu as pltpu

F32 = jnp.float32
BF16 = jnp.bfloat16

RET_HEADS = 4
MLSTM_HEADS = 4
HEAD_DIM = 128
CHUNK = 128
CONV_WIDTH = 4
ROPE_BASE = 10000.0
EPS = 1e-6
D_RET = RET_HEADS * HEAD_DIM
D_MLSTM = MLSTM_HEADS * HEAD_DIM

LANES = 128
SUBLANES = 8
MXU_COLS = 256
VMEM_LIMIT_BYTES = 56 * 1024 * 1024

COL_RQ = 0
COL_RK = D_RET
COL_RV = 2 * D_RET
COL_RG = 3 * D_RET
COL_MQK = 4 * D_RET
COL_MV = COL_MQK + 2 * D_MLSTM
COL_MO = COL_MV + D_MLSTM
COL_GATES = COL_MO + D_MLSTM

NT_DIMS = (((1,), (1,)), ((), ()))


def _tiles(seq):
    ffn_rows = min(1024, seq)
    mix_rows = min(512, seq)
    prepare_rows = seq // 2
    assert seq % ffn_rows == 0 and seq % mix_rows == 0 and mix_rows % CHUNK == 0
    return ffn_rows, mix_rows, prepare_rows


def _dot(a, b):
    return jnp.dot(a, b, preferred_element_type=F32)


def _rmsnorm(x, g):
    return x * lax.rsqrt(jnp.mean(x * x, axis=-1, keepdims=True) + EPS) * g


def _modulated_norm(x, mod, g):
    shift, scale = mod[0:1], mod[1:2]
    return _rmsnorm(x, g) * (1.0 + scale) + shift


def _log_sigmoid(x):
    return jnp.minimum(x, 0.0) - jnp.log1p(jnp.exp(-jnp.abs(x)))


def _fold_to_tile(t):
    acc = None
    for r0 in range(0, t.shape[0], SUBLANES):
        for c0 in range(0, t.shape[1], LANES):
            blk = t[r0:r0 + SUBLANES, c0:c0 + LANES]
            acc = blk if acc is None else acc + blk
    return acc


def _ada_kernel(c_ref, w_ref, b_ref, o_ref):
    sc = jax.nn.silu(c_ref[...])
    w = w_ref[...]
    s_hi = sc.astype(BF16)
    s_lo = (sc - s_hi.astype(F32)).astype(BF16)
    w_hi = w.astype(BF16)
    w_lo = (w - w_hi.astype(F32)).astype(BF16)
    n = sc.shape[0]
    both = _dot(jnp.concatenate([s_hi, s_lo], axis=0), w_hi)
    o_ref[...] = both[:n] + both[n:] + _dot(s_hi, w_lo) + b_ref[...]


def _ada_mod(c, w_ada, b_ada):
    depth, nsub, d, d3 = w_ada.shape
    n = depth * nsub
    b = c.shape[0]
    out = pl.pallas_call(
        _ada_kernel,
        grid=(n, d3 // d),
        in_specs=[
            pl.BlockSpec((b, d), lambda i, j: (0, 0)),
            pl.BlockSpec((None, d, d), lambda i, j: (i, 0, j)),
            pl.BlockSpec((None, 1, d), lambda i, j: (i, 0, j)),
        ],
        out_specs=pl.BlockSpec((None, b, d), lambda i, j: (i, 0, j)),
        out_shape=jax.ShapeDtypeStruct((n, b, d3), F32),
        compiler_params=pltpu.CompilerParams(
            dimension_semantics=("arbitrary", "arbitrary"), vmem_limit_bytes=VMEM_LIMIT_BYTES),
        name="ada_mod",
    )(c, w_ada.reshape(n, d, d3), b_ada.reshape(n, 1, d3))
    return out.reshape(depth, nsub, b, d3 // d, d)


def _prepare_kernel(pos_ref, invf_ref, *refs):
    n_weights = (len(refs) - 2) // 2
    weight_refs = refs[:n_weights]
    cos_ref, sin_ref = refs[n_weights:n_weights + 2]
    for src_ref, dst_ref in zip(weight_refs, refs[n_weights + 2:]):
        dst_ref[...] = src_ref[...].astype(BF16)
    rows, half = pos_ref.shape[0], HEAD_DIM // 2
    pos = pos_ref[...].astype(F32)
    low = lax.broadcasted_iota(jnp.int32, (rows // 2, HEAD_DIM), 1) < half
    ang = jnp.where(low, pos[:rows // 2], pos[rows // 2:]) * invf_ref[...]
    c, s = jnp.cos(ang), jnp.sin(ang)
    c_swapped, s_swapped = pltpu.roll(c, half, 1), pltpu.roll(s, half, 1)
    cos_ref[:rows // 2, :] = jnp.where(low, c, c_swapped)
    cos_ref[rows // 2:, :] = jnp.where(low, c_swapped, c)
    sin_ref[:rows // 2, :] = jnp.where(low, -s, s_swapped)
    sin_ref[rows // 2:, :] = jnp.where(low, -s_swapped, s)


def _prepare(positions, rows, weights):
    b, s = positions.shape
    assert rows % (2 * SUBLANES) == 0 and s % rows == 0
    n_j = s // rows
    n_steps = b * n_j
    inv_freq = ROPE_BASE ** (-jnp.arange(0, HEAD_DIM, 2, dtype=F32) / HEAD_DIM)
    invf = jnp.concatenate([inv_freq, inv_freq]).reshape(1, HEAD_DIM)
    table = jax.ShapeDtypeStruct((b, s, HEAD_DIM), F32)
    const = pl.BlockSpec((1, HEAD_DIM), lambda i, j: (0, 0))
    tile = pl.BlockSpec((None, rows, HEAD_DIM), lambda i, j: (i, j, 0))
    flat, weight_specs, cast_shapes = [], [], []
    for w, n_cols in weights:
        w2d = w.reshape(-1, w.shape[-1])
        block_rows = w2d.shape[0] // n_steps
        packed = 2 * SUBLANES
        assert block_rows * n_steps == w2d.shape[0] and block_rows % packed == 0
        assert n_cols % LANES == 0
        flat.append(w2d)
        weight_specs.append(pl.BlockSpec((block_rows, n_cols), lambda i, j: (i * n_j + j, 0)))
        cast_shapes.append(jax.ShapeDtypeStruct((w2d.shape[0], n_cols), BF16))
    outs = pl.pallas_call(
        _prepare_kernel,
        grid=(b, n_j),
        in_specs=[pl.BlockSpec((None, rows, 1), lambda i, j: (i, j, 0)), const] + weight_specs,
        out_specs=[tile, tile] + weight_specs,
        out_shape=[table, table] + cast_shapes,
        compiler_params=pltpu.CompilerParams(
            dimension_semantics=("arbitrary", "arbitrary"), vmem_limit_bytes=VMEM_LIMIT_BYTES),
        name="prepare",
    )(positions.reshape(b, s, 1), invf, *flat)
    casts = [o.reshape(*w.shape[:-1], n_cols) for o, (w, n_cols) in zip(outs[2:], weights)]
    return outs[0], outs[1], casts


NORM_PIECES = 8


def _ffn_kernel(x_ref, xnext_ref, mod_ref, modnext_ref, g_ref, w1_ref, w3_ref, w2_ref, *rest,
                final_norm):
    if final_norm:
        gf_ref, o_ref, h_even_ref, h_odd_ref, act_ref = rest
    else:
        o_ref, h_even_ref, h_odd_ref, act_ref = rest
    t = pl.program_id(0)
    rows = x_ref.shape[0]
    d_ff = w1_ref.shape[1]

    def normalise(src_ref, src_mod_ref, dst_ref, r0, n):
        hn = _modulated_norm(src_ref[r0:r0 + n, :], src_mod_ref[...], g_ref[...])
        dst_ref[r0:r0 + n, :] = hn.astype(BF16)
        return hn

    @pl.when(t == 0)
    def _first_tile():
        normalise(x_ref, mod_ref, h_even_ref, 0, rows)

    def tile_step(h_ref, hnext_ref):
        piece = rows // NORM_PIECES
        pieces = list(range(0, rows, piece))
        anchor = None
        for c0 in range(0, d_ff, MXU_COLS):
            h = h_ref[...]
            a = _dot(h, w1_ref[:, c0:c0 + MXU_COLS])
            b = _dot(h, w3_ref[:, c0:c0 + MXU_COLS])
            act = jax.nn.silu(a) * b
            act_ref[:, c0:c0 + MXU_COLS] = act.astype(BF16)
            if anchor is not None:
                packed = 2 * SUBLANES
                act_ref[0:packed, c0:c0 + LANES] = (
                    act[0:packed, 0:LANES]
                    + 0.0 * jnp.concatenate([anchor, anchor], axis=0)).astype(BF16)
                anchor = None
            if pieces:
                anchor = _fold_to_tile(
                    normalise(xnext_ref, modnext_ref, hnext_ref, pieces.pop(0), piece))
        assert not pieces and anchor is None
        out = x_ref[...] + 0.5 * mod_ref[2:3, :] * _dot(act_ref[...], w2_ref[...])
        if final_norm:
            out = _rmsnorm(out, gf_ref[...])
        o_ref[...] = out

    @pl.when(t % 2 == 0)
    def _even_tile():
        tile_step(h_even_ref, h_odd_ref)

    @pl.when(t % 2 == 1)
    def _odd_tile():
        tile_step(h_odd_ref, h_even_ref)


def _ffn(x, mod, g, which, w1, w3, w2, rows, g_final=None):
    b, s, d = x.shape
    d_ff = w1.shape[-1]
    assert d_ff % MXU_COLS == 0 and rows % (NORM_PIECES * SUBLANES) == 0
    final_norm = g_final is not None
    tiles_per_seq = s // rows
    n_tiles = b * tiles_per_seq
    following = lambda t: jnp.minimum(t + 1, n_tiles - 1)
    resident = lambda shape: pl.BlockSpec(shape, lambda t: (0, 0), pipeline_mode=pl.Buffered(1))
    weight = lambda rows_, cols_: pl.BlockSpec((None, None, rows_, cols_),
                                               lambda t: (*which, 0, 0),
                                               pipeline_mode=pl.Buffered(1))
    tile = pl.BlockSpec((None, rows, d), lambda t: (t, 0, 0))
    in_specs = [
        tile,
        pl.BlockSpec((None, rows, d), lambda t: (following(t), 0, 0)),
        pl.BlockSpec((None, 3, d), lambda t: (t // tiles_per_seq, 0, 0)),
        pl.BlockSpec((None, 3, d), lambda t: (following(t) // tiles_per_seq, 0, 0)),
        resident((1, d)),
        weight(d, d_ff),
        weight(d, d_ff),
        weight(d_ff, d),
    ]
    x_tiles = x.reshape(n_tiles, rows, d)
    args = [x_tiles, x_tiles, mod, mod, g.reshape(1, d), w1, w3, w2]
    if final_norm:
        in_specs.append(resident((1, d)))
        args.append(g_final.reshape(1, d))
    out = pl.pallas_call(
        functools.partial(_ffn_kernel, final_norm=final_norm),
        grid=(n_tiles,),
        in_specs=in_specs,
        out_specs=tile,
        out_shape=jax.ShapeDtypeStruct(x_tiles.shape, x.dtype),
        scratch_shapes=[pltpu.VMEM((rows, d), BF16), pltpu.VMEM((rows, d), BF16),
                        pltpu.VMEM((rows, d_ff), BF16)],
        compiler_params=pltpu.CompilerParams(
            dimension_semantics=("arbitrary",), vmem_limit_bytes=VMEM_LIMIT_BYTES),
        name="ffn_final" if final_norm else "ffn",
    )(*args)
    return out.reshape(x.shape)


Z_RQ = 0
Z_RK = D_RET
Z_RG = 2 * D_RET
Z_MO = 3 * D_RET
Z_WIDTH = 3 * D_RET + D_MLSTM


def _head_norm(h, g):
    mu = jnp.mean(h, axis=-1, keepdims=True)
    d = h - mu
    var = jnp.mean(d * d, axis=-1, keepdims=True)
    return d * lax.rsqrt(var + EPS) * g


def _prefix_scan_rows(t, row_id, combine, identity):
    shift = 1
    while shift < CHUNK:
        t = combine(t, jnp.where(row_id >= shift, pltpu.roll(t, shift, 0), identity))
        shift *= 2
    return t


def _stage_retention(r, hh, z_ref, cos, sin, wq_ref, wk_ref, rq_ref, rqw_ref, rk_ref, rkwt_ref):
    r0, c0 = r * CHUNK, hh * HEAD_DIM
    q = z_ref[r0:r0 + CHUNK, Z_RQ + c0:Z_RQ + c0 + HEAD_DIM]
    k = z_ref[r0:r0 + CHUNK, Z_RK + c0:Z_RK + c0 + HEAD_DIM]
    half = HEAD_DIM // 2
    qr = q * cos + pltpu.roll(q, half, 1) * sin
    kr = (k * cos + pltpu.roll(k, half, 1) * sin) * (HEAD_DIM ** -0.5)
    rq_ref[r0:r0 + CHUNK, c0:c0 + HEAD_DIM] = qr.astype(BF16)
    rqw_ref[r0:r0 + CHUNK, c0:c0 + HEAD_DIM] = (qr * wq_ref[hh]).astype(BF16)
    rk_ref[r0:r0 + CHUNK, c0:c0 + HEAD_DIM] = kr.astype(BF16)
    rkwt_ref[r, hh] = (kr * wk_ref[hh]).T.astype(BF16)


def _stage_mlstm_qk(r, blk, qkraw_ref, convw_ref, convb_ref, mq_ref, mk_ref, mkt_ref):
    r0, c0 = r * CHUNK, blk * HEAD_DIM
    acc = convb_ref[:, c0:c0 + HEAD_DIM]
    for j in range(CONV_WIDTH):
        start = SUBLANES - (CONV_WIDTH - 1) + j + r0
        acc = acc + (convw_ref[j:j + 1, c0:c0 + HEAD_DIM]
                     * qkraw_ref[start:start + CHUNK, c0:c0 + HEAD_DIM])
    qk = jax.nn.silu(acc)
    if blk < MLSTM_HEADS:
        mq_ref[r0:r0 + CHUNK, c0:c0 + HEAD_DIM] = (qk * (HEAD_DIM ** -0.5)).astype(BF16)
    else:
        hh = blk - MLSTM_HEADS
        mk_ref[r0:r0 + CHUNK, hh * HEAD_DIM:(hh + 1) * HEAD_DIM] = qk.astype(BF16)
        mkt_ref[r, hh] = qk.T


def _stage_gates(r, gate_ref, row_id, cmb_ref, bb_ref, arow_ref):
    r0 = r * CHUNK
    gates = gate_ref[r0:r0 + CHUNK, :]
    bcum = _prefix_scan_rows(gates, row_id, jnp.add, 0.0)
    a = gates - pltpu.roll(bcum, LANES - MLSTM_HEADS, 1)
    cmax = _prefix_scan_rows(a, row_id, jnp.maximum, -jnp.inf)
    arow_ref[r] = a.T[0:SUBLANES, :]
    for hh in range(MLSTM_HEADS):
        cols = slice(hh * HEAD_DIM, (hh + 1) * HEAD_DIM)
        cmb_ref[r0:r0 + CHUNK, cols] = jnp.broadcast_to(cmax[:, hh:hh + 1], (CHUNK, HEAD_DIM))
        fcol = MLSTM_HEADS + hh
        bb_ref[r0:r0 + CHUNK, cols] = jnp.broadcast_to(bcum[:, fcol:fcol + 1], (CHUNK, HEAD_DIM))


def _retention_chunk(hh, r, rq_ref, rqw_ref, rk_ref, rv_ref, rkwt_ref, decay_ref, cdec_ref,
                     state_ref, pre_ref):
    rows = slice(r * CHUNK, (r + 1) * CHUNK)
    cols = slice(hh * HEAD_DIM, (hh + 1) * HEAD_DIM)
    scores = lax.dot_general(rq_ref[rows, cols], rk_ref[rows, cols], NT_DIMS,
                             preferred_element_type=F32)
    yield
    vb = rv_ref[rows, cols]
    state = state_ref[hh]
    intra = _dot((scores * decay_ref[hh]).astype(BF16), vb)
    inter = _dot(rqw_ref[rows, cols], state.astype(BF16))
    update = _dot(rkwt_ref[r, hh], vb)
    yield
    pre_ref[rows, cols] = intra + inter
    state_ref[hh] = state * cdec_ref[hh] + update


def _mlstm_chunk(hh, r, causal, mq_ref, mk_ref, mv_ref, mkt_ref, cmb_ref, bb_ref, arow_ref,
                 c_ref, m_ref, pre_ref):
    rows = slice(r * CHUNK, (r + 1) * CHUNK)
    c0 = hh * HEAD_DIM
    cols = slice(c0, c0 + HEAD_DIM)
    qb = mq_ref[rows, cols]
    scores = lax.dot_general(qb, mk_ref[rows, cols], NT_DIMS, preferred_element_type=F32)
    yield
    v_aug = mv_ref[rows, 2 * c0:2 * c0 + 2 * HEAD_DIM]
    cmax = cmb_ref[rows, cols]
    b_l = bb_ref[rows, cols]
    a_row = arow_ref[r][hh:hh + 1, :]
    m_prev = m_ref[hh]
    state = c_ref[hh]
    mx = jnp.maximum(cmax, m_prev)
    w_d = jnp.exp(jnp.where(causal, a_row - mx, -jnp.inf))
    intra = _dot((scores * w_d).astype(BF16), v_aug)
    inter = _dot(qb, state.astype(BF16))
    b_last = b_l[CHUNK - 1:CHUNK, :]
    m_new = jnp.maximum(b_last + m_prev, b_last + cmax[CHUNK - 1:CHUNK, :])
    w_g = jnp.exp(b_last + a_row - m_new)
    update = _dot((mkt_ref[r, hh] * w_g).astype(BF16), v_aug)
    yield
    w_inter = jnp.exp(m_prev - mx)
    tot = intra + jnp.concatenate([w_inter, w_inter], axis=1) * inter
    num, den = tot[:, :HEAD_DIM], tot[:, HEAD_DIM:]
    h = num / jnp.maximum(jnp.abs(den), jnp.exp(-(b_l + mx)))
    pre_ref[rows, D_RET + c0:D_RET + c0 + HEAD_DIM] = h
    w_c = jnp.exp(b_last + m_prev - m_new)
    c_ref[hh] = jnp.concatenate([w_c, w_c], axis=1) * state + update
    m_ref[hh] = m_new


def _run_staggered(tasks, stages, fillers):
    n = len(tasks)
    n_steps = n + stages - 1
    for step in range(n_steps):
        for run_filler in fillers.get(step, ()):
            run_filler()
        for s in range(stages):
            j = step - s
            if 0 <= j < n:
                next(tasks[j], None)
    assert all(step < n_steps for step in fillers)


def _mixer_kernel(x_ref, mod_ref, g_ref, win_ref, wgate_ref, gbias_ref, cos_ref, sin_ref, convw_ref,
                  convb_ref,
                  decay_ref, wq_ref, wk_ref, cdec_ref, gret_ref, gml_ref, wout_ref,
                  o_ref,
                  h_ref, z_ref, qkraw_ref, gate_ref, pre_ref, y_ref,
                  rq_ref, rqw_ref, rk_ref, rv_ref, rkwt_ref, mq_ref, mk_ref, mv_ref, mkt_ref,
                  cmb_ref, bb_ref, arow_ref, sret_ref, c_ref, m_ref):
    rows_total = x_ref.shape[0]
    n_chunks = rows_total // CHUNK

    @pl.when(pl.program_id(1) == 0)
    def _start_of_sequence():
        sret_ref[...] = jnp.zeros_like(sret_ref)
        c_ref[...] = jnp.zeros_like(c_ref)
        m_ref[...] = jnp.zeros_like(m_ref)
        qkraw_ref[0:SUBLANES, :] = jnp.zeros((SUBLANES, qkraw_ref.shape[1]), F32)
        for hh in range(MLSTM_HEADS):
            ones_cols = slice((2 * hh + 1) * HEAD_DIM, (2 * hh + 2) * HEAD_DIM)
            mv_ref[:, ones_cols] = jnp.ones((rows_total, HEAD_DIM), BF16)

    mod = mod_ref[...]
    h_ref[...] = _modulated_norm(x_ref[...], mod, g_ref[...]).astype(BF16)

    def project(c0, width=D_RET):
        return _dot(h_ref[...], win_ref[:, c0:c0 + width])

    row_id = lax.broadcasted_iota(jnp.int32, (CHUNK, CHUNK), 0)
    col_id = lax.broadcasted_iota(jnp.int32, (CHUNK, CHUNK), 1)
    causal = row_id >= col_id

    def project_to_z(col, zcol, width=D_RET):
        z_ref[:, zcol:zcol + width] = project(col, width)

    def project_qk(c0):
        qkraw_ref[SUBLANES:SUBLANES + rows_total, c0:c0 + D_RET] = project(COL_MQK + c0)

    def project_gates():
        zg = _dot(h_ref[...], wgate_ref[...]) + gbias_ref[...]
        lane = lax.broadcasted_iota(jnp.int32, zg.shape, 1)
        gate_ref[...] = jnp.where(lane < MLSTM_HEADS, zg, _log_sigmoid(zg))

    def project_rv():
        rv_ref[...] = project(COL_RV).astype(BF16)

    def project_mv():
        mv = project(COL_MV)
        for hh in range(MLSTM_HEADS):
            mv_ref[:, 2 * hh * HEAD_DIM:(2 * hh + 1) * HEAD_DIM] = (
                mv[:, hh * HEAD_DIM:(hh + 1) * HEAD_DIM].astype(BF16))

    def stage_retention(r):
        cos = cos_ref[r * CHUNK:(r + 1) * CHUNK, :]
        sin = sin_ref[r * CHUNK:(r + 1) * CHUNK, :]
        for hh in range(RET_HEADS):
            _stage_retention(r, hh, z_ref, cos, sin, wq_ref, wk_ref, rq_ref, rqw_ref, rk_ref,
                             rkwt_ref)

    def stage_mlstm_q(r):
        for blk in range(MLSTM_HEADS):
            _stage_mlstm_qk(r, blk, qkraw_ref, convw_ref, convb_ref, mq_ref, mk_ref, mkt_ref)

    def stage_mlstm_k(r):
        for blk in range(MLSTM_HEADS, 2 * MLSTM_HEADS):
            _stage_mlstm_qk(r, blk, qkraw_ref, convw_ref, convb_ref, mq_ref, mk_ref, mkt_ref)

    def stage_gates(r):
        _stage_gates(r, gate_ref, row_id, cmb_ref, bb_ref, arow_ref)

    chunks = range(n_chunks)
    half = (n_chunks + 1) // 2
    emission = [
        ([functools.partial(project_qk, 0)], []),
        ([functools.partial(project_qk, D_RET)], [(stage_mlstm_q, r) for r in chunks[:half]]),
        ([project_gates, functools.partial(project_to_z, COL_RQ, Z_RQ)],
         [(stage_mlstm_q, r) for r in chunks[half:]]),
        ([functools.partial(project_to_z, COL_RK, Z_RK)],
         [(stage_mlstm_k, r) for r in chunks[:half]]),
        ([project_rv], [(stage_mlstm_k, r) for r in chunks[half:]] + [(stage_gates, r) for r in chunks]),
        ([project_mv], [(stage_retention, 0)]),
    ]
    for run_projections, run_stages in emission:
        for run_projection in run_projections:
            run_projection()
        for stage, r in run_stages:
            stage(r)
    qkraw_ref[0:SUBLANES, :] = qkraw_ref[rows_total:rows_total + SUBLANES, :]

    tasks = []
    for r in range(n_chunks):
        for hh in range(max(RET_HEADS, MLSTM_HEADS)):
            if hh < MLSTM_HEADS:
                tasks.append(_mlstm_chunk(hh, r, causal, mq_ref, mk_ref, mv_ref, mkt_ref,
                                          cmb_ref, bb_ref, arow_ref, c_ref, m_ref, pre_ref))
            if hh < RET_HEADS:
                tasks.append(_retention_chunk(hh, r, rq_ref, rqw_ref, rk_ref, rv_ref, rkwt_ref,
                                              decay_ref, cdec_ref, sret_ref, pre_ref))
    n_stages = 3
    n_steps = len(tasks) + n_stages - 1
    tasks_per_chunk = len(tasks) // n_chunks
    fillers = {}
    for r in range(1, n_chunks):
        fillers.setdefault((r - 1) * tasks_per_chunk, []).append(
            functools.partial(stage_retention, r))
    late = [(col + c0, zcol + c0) for col, zcol in ((COL_RG, Z_RG), (COL_MO, Z_MO))
            for c0 in range(0, D_RET, MXU_COLS)]
    for k, (col, zcol) in enumerate(late):
        fillers.setdefault(1 + k * (n_steps // len(late)), []).append(
            functools.partial(project_to_z, col, zcol, MXU_COLS))
    _run_staggered(tasks, n_stages, fillers)

    for hh in range(RET_HEADS + MLSTM_HEADS):
        is_retention = hh < RET_HEADS
        c0 = (hh if is_retention else hh - RET_HEADS) * HEAD_DIM
        pre = pre_ref[:, hh * HEAD_DIM:(hh + 1) * HEAD_DIM]
        if is_retention:
            hn = (jax.nn.silu(z_ref[:, Z_RG + c0:Z_RG + c0 + HEAD_DIM])
                  * _head_norm(pre, gret_ref[:, c0:c0 + HEAD_DIM]))
        else:
            gate = jax.nn.sigmoid(z_ref[:, Z_MO + c0:Z_MO + c0 + HEAD_DIM])
            hn = _head_norm(gate * pre, gml_ref[:, c0:c0 + HEAD_DIM])
        y_ref[:, hh * HEAD_DIM:(hh + 1) * HEAD_DIM] = hn.astype(y_ref.dtype)
    o_ref[...] = x_ref[...] + mod[2:3] * _dot(y_ref[...], wout_ref[...])


def _retention_constants():
    heads = jnp.arange(RET_HEADS, dtype=F32)
    log_gamma = jnp.log(1.0 - 2.0 ** (-5.0 - heads))
    idx = jnp.arange(CHUNK)
    diff = (idx[:, None] - idx[None, :]).astype(F32)
    decay = jnp.where(diff >= 0, jnp.exp(log_gamma[:, None, None] * jnp.maximum(diff, 0.0)), 0.0)
    w_k = jnp.exp(log_gamma[:, None] * (CHUNK - 1 - idx).astype(F32))
    w_q = jnp.exp(log_gamma[:, None] * (idx + 1).astype(F32))
    chunk_decay = jnp.exp(log_gamma * CHUNK)
    bcast = lambda t: jnp.broadcast_to(t[:, :, None], (RET_HEADS, CHUNK, HEAD_DIM))
    cdec = jnp.broadcast_to(chunk_decay[:, None, None], (RET_HEADS, 1, HEAD_DIM))
    return decay, bcast(w_q), bcast(w_k), cdec


def _gate_weights(w_in):
    n_gates = w_in.shape[-1] - COL_GATES
    return jnp.pad(w_in[:, :, COL_GATES:], ((0, 0), (0, 0), (0, LANES - n_gates))).astype(BF16)


def _mixer(x, mod, g, layer, w_main, w_gate, conv_w, conv_b, b_igate, b_fgate, g_ret, g_ml, w_out,
           cos, sin, rows):
    b, s, d = x.shape
    d_mix = D_RET + D_MLSTM
    n_gates = 2 * MLSTM_HEADS
    n_chunks = rows // CHUNK
    assert w_main.shape[1] == d and w_main.shape[2] >= COL_GATES and w_out.shape[1:] == (d_mix, d)
    gbias = jnp.pad(jnp.concatenate([b_igate, b_fgate]), (0, LANES - n_gates)).reshape(1, LANES)
    decay, w_q, w_k, cdec = _retention_constants()

    def resident(shape):
        zeros = (0,) * len(shape)
        return pl.BlockSpec(shape, lambda i, j: zeros, pipeline_mode=pl.Buffered(1))

    def layer_weight(rows_, cols_):
        return pl.BlockSpec((None, rows_, cols_), lambda i, j: (layer, 0, 0),
                            pipeline_mode=pl.Buffered(1))

    tile = lambda width: pl.BlockSpec((None, rows, width), lambda i, j: (i, j, 0))
    in_specs = [
        tile(d),
        pl.BlockSpec((None, 3, d), lambda i, j: (i, 0, 0)),
        resident((1, d)),
        layer_weight(d, COL_GATES),
        layer_weight(d, LANES),
        resident((1, LANES)),
        tile(HEAD_DIM),
        tile(HEAD_DIM),
        resident((CONV_WIDTH, 2 * D_MLSTM)),
        resident((1, 2 * D_MLSTM)),
        resident((RET_HEADS, CHUNK, CHUNK)),
        resident((RET_HEADS, CHUNK, HEAD_DIM)),
        resident((RET_HEADS, CHUNK, HEAD_DIM)),
        resident((RET_HEADS, 1, HEAD_DIM)),
        resident((1, D_RET)),
        resident((1, D_MLSTM)),
        layer_weight(d_mix, d),
    ]
    head_tiles = lambda heads, dtype: pltpu.VMEM((n_chunks, heads, HEAD_DIM, CHUNK), dtype)
    scratch_shapes = [
        pltpu.VMEM((rows, d), BF16),
        pltpu.VMEM((rows, Z_WIDTH), F32),
        pltpu.VMEM((rows + SUBLANES, 2 * D_MLSTM), F32),
        pltpu.VMEM((rows, LANES), F32),
        pltpu.VMEM((rows, d_mix), F32),
        pltpu.VMEM((rows, d_mix), BF16),
        pltpu.VMEM((rows, D_RET), BF16),
        pltpu.VMEM((rows, D_RET), BF16),
        pltpu.VMEM((rows, D_RET), BF16),
        pltpu.VMEM((rows, D_RET), BF16),
        head_tiles(RET_HEADS, BF16),
        pltpu.VMEM((rows, D_MLSTM), BF16),
        pltpu.VMEM((rows, D_MLSTM), BF16),
        pltpu.VMEM((rows, 2 * D_MLSTM), BF16),
        head_tiles(MLSTM_HEADS, F32),
        pltpu.VMEM((rows, D_MLSTM), F32),
        pltpu.VMEM((rows, D_MLSTM), F32),
        pltpu.VMEM((n_chunks, SUBLANES, CHUNK), F32),
        pltpu.VMEM((RET_HEADS, HEAD_DIM, HEAD_DIM), F32),
        pltpu.VMEM((MLSTM_HEADS, HEAD_DIM, 2 * HEAD_DIM), F32),
        pltpu.VMEM((MLSTM_HEADS, 1, LANES), F32),
    ]
    return pl.pallas_call(
        _mixer_kernel,
        grid=(b, s // rows),
        in_specs=in_specs,
        out_specs=tile(d),
        out_shape=jax.ShapeDtypeStruct(x.shape, x.dtype),
        scratch_shapes=scratch_shapes,
        compiler_params=pltpu.CompilerParams(
            dimension_semantics=("arbitrary", "arbitrary"), vmem_limit_bytes=VMEM_LIMIT_BYTES),
        name="mixer",
    )(x, mod, g.reshape(1, d), w_main, w_gate, gbias, cos, sin, conv_w, conv_b.reshape(1, -1),
      decay, w_q, w_k, cdec, g_ret.reshape(1, -1), g_ml.reshape(1, -1), w_out)


@jax.jit
def kernel(x, c, positions, norm_g, w_ada, b_ada, w_ff1, w_ff3, w_ff2, w_in, conv_w, conv_b,
           b_igate, b_fgate, g_ret_norm, g_mlstm_norm, w_out, g_final):
    depth = w_in.shape[0]
    ffn_rows, mix_rows, prepare_rows = _tiles(x.shape[1])
    mods = _ada_mod(c, w_ada, b_ada)
    d, d_ff = w_ff1.shape[-2:]
    cos, sin, (*ffn_weights, w_out) = _prepare(
        positions, prepare_rows, [(w_ff1, d_ff), (w_ff3, d_ff), (w_ff2, d), (w_out, d)])
    w_main = w_in.astype(BF16)
    w_gate = _gate_weights(w_main)
    for l in range(depth):
        x = _ffn(x, mods[l, 0], norm_g[l, 0], (l, 0), *ffn_weights, ffn_rows)
        x = _mixer(x, mods[l, 1], norm_g[l, 1], l, w_main, w_gate, conv_w[l], conv_b[l],
                   b_igate[l], b_fgate[l], g_ret_norm[l], g_mlstm_norm[l], w_out, cos, sin,
                   mix_rows)
        x = _ffn(x, mods[l, 2], norm_g[l, 2], (l, 1), *ffn_weights, ffn_rows,
                 g_final=g_final if l == depth - 1 else None)
    return x
```

```python
import functools

import jax
import jax.numpy as jnp
from jax import lax
from jax.experimental import pallas as pl
from jax.experimental.pallas import tpu as pltpu

F32 = jnp.float32
BF16 = jnp.bfloat16

RET_HEADS = 4
MLSTM_HEADS = 4
HEAD_DIM = 128
CHUNK = 128
CONV_WIDTH = 4
ROPE_BASE = 10000.0
EPS = 1e-6
D_RET = RET_HEADS * HEAD_DIM
D_MLSTM = MLSTM_HEADS * HEAD_DIM

LANES = 128
SUBLANES = 8
MXU_COLS = 256
VMEM_LIMIT_BYTES = 60 * 1024 * 1024

COL_RQ = 0
COL_RK = D_RET
COL_RV = 2 * D_RET
COL_RG = 3 * D_RET
COL_MQK = 4 * D_RET
COL_MV = COL_MQK + 2 * D_MLSTM
COL_MO = COL_MV + D_MLSTM
COL_GATES = COL_MO + D_MLSTM

NT_DIMS = (((1,), (1,)), ((), ()))


def _tiles(seq):
    ffn_rows = min(1024, seq)
    mix_rows = min(1024, seq)
    prepare_rows = seq // 2
    assert seq % ffn_rows == 0 and seq % mix_rows == 0 and mix_rows % CHUNK == 0
    return ffn_rows, mix_rows, prepare_rows


def _dot(a, b):
    return jnp.dot(a, b, preferred_element_type=F32)


def _rmsnorm(x, g):
    return x * lax.rsqrt(jnp.mean(x * x, axis=-1, keepdims=True) + EPS) * g


def _modulated_norm(x, mod, g):
    shift, scale = mod[0:1], mod[1:2]
    return _rmsnorm(x, g) * (1.0 + scale) + shift


def _log_sigmoid(x):
    return jnp.minimum(x, 0.0) - jnp.log1p(jnp.exp(-jnp.abs(x)))


def _fold_to_tile(t):
    acc = None
    for r0 in range(0, t.shape[0], SUBLANES):
        for c0 in range(0, t.shape[1], LANES):
            blk = t[r0:r0 + SUBLANES, c0:c0 + LANES]
            acc = blk if acc is None else acc + blk
    return acc


def _ada_kernel(c_ref, w_ref, b_ref, o_ref):
    sc = jax.nn.silu(c_ref[...])
    w = w_ref[...]
    s_hi = sc.astype(BF16)
    s_lo = (sc - s_hi.astype(F32)).astype(BF16)
    w_hi = w.astype(BF16)
    w_lo = (w - w_hi.astype(F32)).astype(BF16)
    n = sc.shape[0]
    both = _dot(jnp.concatenate([s_hi, s_lo], axis=0), w_hi)
    o_ref[...] = both[:n] + both[n:] + _dot(s_hi, w_lo) + b_ref[...]


def _ada_mod(c, w_ada, b_ada):
    depth, nsub, d, d3 = w_ada.shape
    n = depth * nsub
    b = c.shape[0]
    out = pl.pallas_call(
        _ada_kernel,
        grid=(n, d3 // d),
        in_specs=[
            pl.BlockSpec((b, d), lambda i, j: (0, 0)),
            pl.BlockSpec((None, d, d), lambda i, j: (i, 0, j)),
            pl.BlockSpec((None, 1, d), lambda i, j: (i, 0, j)),
        ],
        out_specs=pl.BlockSpec((None, b, d), lambda i, j: (i, 0, j)),
        out_shape=jax.ShapeDtypeStruct((n, b, d3), F32),
        compiler_params=pltpu.CompilerParams(
            dimension_semantics=("arbitrary", "arbitrary"), vmem_limit_bytes=VMEM_LIMIT_BYTES),
        name="ada_mod",
    )(c, w_ada.reshape(n, d, d3), b_ada.reshape(n, 1, d3))
    return out.reshape(depth, nsub, b, d3 // d, d)


def _prepare_kernel(pos_ref, invf_ref, *refs):
    n_weights = (len(refs) - 2) // 2
    weight_refs = refs[:n_weights]
    cos_ref, sin_ref = refs[n_weights:n_weights + 2]
    for src_ref, dst_ref in zip(weight_refs, refs[n_weights + 2:]):
        dst_ref[...] = src_ref[...].astype(BF16)
    rows, half = pos_ref.shape[0], HEAD_DIM // 2
    pos = pos_ref[...].astype(F32)
    low = lax.broadcasted_iota(jnp.int32, (rows // 2, HEAD_DIM), 1) < half
    ang = jnp.where(low, pos[:rows // 2], pos[rows // 2:]) * invf_ref[...]
    c, s = jnp.cos(ang), jnp.sin(ang)
    c_swapped, s_swapped = pltpu.roll(c, half, 1), pltpu.roll(s, half, 1)
    cos_ref[:rows // 2, :] = jnp.where(low, c, c_swapped)
    cos_ref[rows // 2:, :] = jnp.where(low, c_swapped, c)
    sin_ref[:rows // 2, :] = jnp.where(low, -s, s_swapped)
    sin_ref[rows // 2:, :] = jnp.where(low, -s_swapped, s)


def _prepare(positions, rows, weights):
    b, s = positions.shape
    assert rows % (2 * SUBLANES) == 0 and s % rows == 0
    n_j = s // rows
    n_steps = b * n_j
    inv_freq = ROPE_BASE ** (-jnp.arange(0, HEAD_DIM, 2, dtype=F32) / HEAD_DIM)
    invf = jnp.concatenate([inv_freq, inv_freq]).reshape(1, HEAD_DIM)
    table = jax.ShapeDtypeStruct((b, s, HEAD_DIM), F32)
    const = pl.BlockSpec((1, HEAD_DIM), lambda i, j: (0, 0))
    tile = pl.BlockSpec((None, rows, HEAD_DIM), lambda i, j: (i, j, 0))
    flat, weight_specs, cast_shapes = [], [], []
    for w, n_cols in weights:
        w2d = w.reshape(-1, w.shape[-1])
        block_rows = w2d.shape[0] // n_steps
        packed = 2 * SUBLANES
        assert block_rows * n_steps == w2d.shape[0] and block_rows % packed == 0
        assert n_cols % LANES == 0
        flat.append(w2d)
        weight_specs.append(pl.BlockSpec((block_rows, n_cols), lambda i, j: (i * n_j + j, 0)))
        cast_shapes.append(jax.ShapeDtypeStruct((w2d.shape[0], n_cols), BF16))
    outs = pl.pallas_call(
        _prepare_kernel,
        grid=(b, n_j),
        in_specs=[pl.BlockSpec((None, rows, 1), lambda i, j: (i, j, 0)), const] + weight_specs,
        out_specs=[tile, tile] + weight_specs,
        out_shape=[table, table] + cast_shapes,
        compiler_params=pltpu.CompilerParams(
            dimension_semantics=("arbitrary", "arbitrary"), vmem_limit_bytes=VMEM_LIMIT_BYTES),
        name="prepare",
    )(positions.reshape(b, s, 1), invf, *flat)
    casts = [o.reshape(*w.shape[:-1], n_cols) for o, (w, n_cols) in zip(outs[2:], weights)]
    return outs[0], outs[1], casts


NORM_PIECES = 8


def _ffn_kernel(x_ref, xnext_ref, mod_ref, modnext_ref, g_ref, w1_ref, w3_ref, w2_ref, *rest,
                final_norm):
    if final_norm:
        gf_ref, o_ref, h_even_ref, h_odd_ref, act_ref = rest
    else:
        o_ref, h_even_ref, h_odd_ref, act_ref = rest
    t = pl.program_id(0)
    rows = x_ref.shape[0]
    d_ff = w1_ref.shape[1]

    def normalise(src_ref, src_mod_ref, dst_ref, r0, n):
        hn = _modulated_norm(src_ref[r0:r0 + n, :], src_mod_ref[...], g_ref[...])
        dst_ref[r0:r0 + n, :] = hn.astype(BF16)
        return hn

    @pl.when(t == 0)
    def _first_tile():
        normalise(x_ref, mod_ref, h_even_ref, 0, rows)

    def tile_step(h_ref, hnext_ref):
        piece = rows // NORM_PIECES
        pieces = list(range(0, rows, piece))
        anchor = None
        for c0 in range(0, d_ff, MXU_COLS):
            h = h_ref[...]
            a = _dot(h, w1_ref[:, c0:c0 + MXU_COLS])
            b = _dot(h, w3_ref[:, c0:c0 + MXU_COLS])
            act = jax.nn.silu(a) * b
            act_ref[:, c0:c0 + MXU_COLS] = act.astype(BF16)
            if anchor is not None:
                packed = 2 * SUBLANES
                act_ref[0:packed, c0:c0 + LANES] = (
                    act[0:packed, 0:LANES]
                    + 0.0 * jnp.concatenate([anchor, anchor], axis=0)).astype(BF16)
                anchor = None
            if pieces:
                anchor = _fold_to_tile(
                    normalise(xnext_ref, modnext_ref, hnext_ref, pieces.pop(0), piece))
        assert not pieces and anchor is None
        out = x_ref[...] + 0.5 * mod_ref[2:3, :] * _dot(act_ref[...], w2_ref[...])
        if final_norm:
            out = _rmsnorm(out, gf_ref[...])
        o_ref[...] = out

    @pl.when(t % 2 == 0)
    def _even_tile():
        tile_step(h_even_ref, h_odd_ref)

    @pl.when(t % 2 == 1)
    def _odd_tile():
        tile_step(h_odd_ref, h_even_ref)


def _ffn(x, mod, g, which, w1, w3, w2, rows, g_final=None):
    b, s, d = x.shape
    d_ff = w1.shape[-1]
    assert d_ff % MXU_COLS == 0 and rows % (NORM_PIECES * SUBLANES) == 0
    final_norm = g_final is not None
    tiles_per_seq = s // rows
    n_tiles = b * tiles_per_seq
    following = lambda t: jnp.minimum(t + 1, n_tiles - 1)
    resident = lambda shape: pl.BlockSpec(shape, lambda t: (0, 0), pipeline_mode=pl.Buffered(1))
    weight = lambda rows_, cols_: pl.BlockSpec((None, None, rows_, cols_),
                                               lambda t: (*which, 0, 0),
                                               pipeline_mode=pl.Buffered(1))
    tile = pl.BlockSpec((None, rows, d), lambda t: (t, 0, 0))
    in_specs = [
        tile,
        pl.BlockSpec((None, rows, d), lambda t: (following(t), 0, 0)),
        pl.BlockSpec((None, 3, d), lambda t: (t // tiles_per_seq, 0, 0)),
        pl.BlockSpec((None, 3, d), lambda t: (following(t) // tiles_per_seq, 0, 0)),
        resident((1, d)),
        weight(d, d_ff),
        weight(d, d_ff),
        weight(d_ff, d),
    ]
    x_tiles = x.reshape(n_tiles, rows, d)
    args = [x_tiles, x_tiles, mod, mod, g.reshape(1, d), w1, w3, w2]
    if final_norm:
        in_specs.append(resident((1, d)))
        args.append(g_final.reshape(1, d))
    out = pl.pallas_call(
        functools.partial(_ffn_kernel, final_norm=final_norm),
        grid=(n_tiles,),
        in_specs=in_specs,
        out_specs=tile,
        out_shape=jax.ShapeDtypeStruct(x_tiles.shape, x.dtype),
        scratch_shapes=[pltpu.VMEM((rows, d), BF16), pltpu.VMEM((rows, d), BF16),
                        pltpu.VMEM((rows, d_ff), BF16)],
        compiler_params=pltpu.CompilerParams(
            dimension_semantics=("arbitrary",), vmem_limit_bytes=VMEM_LIMIT_BYTES),
        name="ffn_final" if final_norm else "ffn",
    )(*args)
    return out.reshape(x.shape)


Z_RQ = 0
Z_RK = D_RET
Z_RG = 2 * D_RET
Z_MO = 3 * D_RET
Z_WIDTH = 3 * D_RET + D_MLSTM


def _head_norm(h, g):
    mu = jnp.mean(h, axis=-1, keepdims=True)
    d = h - mu
    var = jnp.mean(d * d, axis=-1, keepdims=True)
    return d * lax.rsqrt(var + EPS) * g


def _prefix_scan_rows(t, row_id, combine, identity):
    shift = 1
    while shift < CHUNK:
        t = combine(t, jnp.where(row_id >= shift, pltpu.roll(t, shift, 0), identity))
        shift *= 2
    return t


def _stage_retention(r, hh, z_ref, cos, sin, wq_ref, wk_ref, rq_ref, rqw_ref, rk_ref, rkwt_ref):
    r0, c0 = r * CHUNK, hh * HEAD_DIM
    q = z_ref[r0:r0 + CHUNK, Z_RQ + c0:Z_RQ + c0 + HEAD_DIM]
    k = z_ref[r0:r0 + CHUNK, Z_RK + c0:Z_RK + c0 + HEAD_DIM]
    half = HEAD_DIM // 2
    qr = q * cos + pltpu.roll(q, half, 1) * sin
    kr = (k * cos + pltpu.roll(k, half, 1) * sin) * (HEAD_DIM ** -0.5)
    rq_ref[r0:r0 + CHUNK, c0:c0 + HEAD_DIM] = qr.astype(BF16)
    rqw_ref[r0:r0 + CHUNK, c0:c0 + HEAD_DIM] = (qr * wq_ref[hh]).astype(BF16)
    rk_ref[r0:r0 + CHUNK, c0:c0 + HEAD_DIM] = kr.astype(BF16)
    rkwt_ref[r, hh] = (kr * wk_ref[hh]).T.astype(BF16)


def _stage_mlstm_qk(r, blk, qkraw_ref, convw_ref, convb_ref, mq_ref, mk_ref, mkt_ref):
    r0, c0 = r * CHUNK, blk * HEAD_DIM
    acc = convb_ref[:, c0:c0 + HEAD_DIM]
    for j in range(CONV_WIDTH):
        start = SUBLANES - (CONV_WIDTH - 1) + j + r0
        acc = acc + (convw_ref[j:j + 1, c0:c0 + HEAD_DIM]
                     * qkraw_ref[start:start + CHUNK, c0:c0 + HEAD_DIM])
    qk = jax.nn.silu(acc)
    if blk < MLSTM_HEADS:
        mq_ref[r0:r0 + CHUNK, c0:c0 + HEAD_DIM] = (qk * (HEAD_DIM ** -0.5)).astype(BF16)
    else:
        hh = blk - MLSTM_HEADS
        mk_ref[r0:r0 + CHUNK, hh * HEAD_DIM:(hh + 1) * HEAD_DIM] = qk.astype(BF16)
        mkt_ref[r, hh] = qk.T


def _stage_gates(r, gate_ref, row_id, cmb_ref, bb_ref, arow_ref):
    r0 = r * CHUNK
    gates = gate_ref[r0:r0 + CHUNK, :]
    bcum = _prefix_scan_rows(gates, row_id, jnp.add, 0.0)
    a = gates - pltpu.roll(bcum, LANES - MLSTM_HEADS, 1)
    cmax = _prefix_scan_rows(a, row_id, jnp.maximum, -jnp.inf)
    arow_ref[r] = a.T[0:SUBLANES, :]
    for hh in range(MLSTM_HEADS):
        cols = slice(hh * HEAD_DIM, (hh + 1) * HEAD_DIM)
        cmb_ref[r0:r0 + CHUNK, cols] = jnp.broadcast_to(cmax[:, hh:hh + 1], (CHUNK, HEAD_DIM))
        fcol = MLSTM_HEADS + hh
        bb_ref[r0:r0 + CHUNK, cols] = jnp.broadcast_to(bcum[:, fcol:fcol + 1], (CHUNK, HEAD_DIM))


def _retention_chunk(hh, r, rq_ref, rqw_ref, rk_ref, rv_ref, rkwt_ref, decay_ref, cdec_ref,
                     state_ref, pre_ref):
    rows = slice(r * CHUNK, (r + 1) * CHUNK)
    cols = slice(hh * HEAD_DIM, (hh + 1) * HEAD_DIM)
    scores = lax.dot_general(rq_ref[rows, cols], rk_ref[rows, cols], NT_DIMS,
                             preferred_element_type=F32)
    yield
    vb = rv_ref[rows, cols]
    state = state_ref[hh]
    intra = _dot((scores * decay_ref[hh]).astype(BF16), vb)
    inter = _dot(rqw_ref[rows, cols], state.astype(BF16))
    update = _dot(rkwt_ref[r, hh], vb)
    yield
    pre_ref[rows, cols] = intra + inter
    state_ref[hh] = state * cdec_ref[hh] + update


def _mlstm_chunk(hh, r, causal, mq_ref, mk_ref, mv_ref, mkt_ref, cmb_ref, bb_ref, arow_ref,
                 c_ref, m_ref, pre_ref):
    rows = slice(r * CHUNK, (r + 1) * CHUNK)
    c0 = hh * HEAD_DIM
    cols = slice(c0, c0 + HEAD_DIM)
    qb = mq_ref[rows, cols]
    scores = lax.dot_general(qb, mk_ref[rows, cols], NT_DIMS, preferred_element_type=F32)
    yield
    v_aug = mv_ref[rows, 2 * c0:2 * c0 + 2 * HEAD_DIM]
    cmax = cmb_ref[rows, cols]
    b_l = bb_ref[rows, cols]
    a_row = arow_ref[r][hh:hh + 1, :]
    m_prev = m_ref[hh]
    state = c_ref[hh]
    mx = jnp.maximum(cmax, m_prev)
    w_d = jnp.exp(jnp.where(causal, a_row - mx, -jnp.inf))
    intra = _dot((scores * w_d).astype(BF16), v_aug)
    inter = _dot(qb, state.astype(BF16))
    b_last = b_l[CHUNK - 1:CHUNK, :]
    m_new = jnp.maximum(b_last + m_prev, b_last + cmax[CHUNK - 1:CHUNK, :])
    w_g = jnp.exp(b_last + a_row - m_new)
    update = _dot((mkt_ref[r, hh] * w_g).astype(BF16), v_aug)
    yield
    w_inter = jnp.exp(m_prev - mx)
    tot = intra + jnp.concatenate([w_inter, w_inter], axis=1) * inter
    num, den = tot[:, :HEAD_DIM], tot[:, HEAD_DIM:]
    h = num / jnp.maximum(jnp.abs(den), jnp.exp(-(b_l + mx)))
    pre_ref[rows, D_RET + c0:D_RET + c0 + HEAD_DIM] = h
    w_c = jnp.exp(b_last + m_prev - m_new)
    c_ref[hh] = jnp.concatenate([w_c, w_c], axis=1) * state + update
    m_ref[hh] = m_new


def _run_staggered(tasks, stages, fillers):
    n = len(tasks)
    n_steps = n + stages - 1
    for step in range(n_steps):
        for run_filler in fillers.get(step, ()):
            run_filler()
        for s in range(stages):
            j = step - s
            if 0 <= j < n:
                next(tasks[j], None)
    assert all(step < n_steps for step in fillers)


def _mixer_kernel(x_ref, mod_ref, g_ref, win_ref, wgate_ref, gbias_ref, cos_ref, sin_ref, convw_ref,
                  convb_ref,
                  decay_ref, wq_ref, wk_ref, cdec_ref, gret_ref, gml_ref, wout_ref,
                  o_ref,
                  h_ref, z_ref, qkraw_ref, gate_ref, y_ref,
                  rq_ref, rqw_ref, rk_ref, rv_ref, rkwt_ref, mq_ref, mk_ref, mv_ref, mkt_ref,
                  cmb_ref, bb_ref, arow_ref, sret_ref, c_ref, m_ref):
    rows_total = x_ref.shape[0]
    n_chunks = rows_total // CHUNK
    assert (Z_RQ, Z_RK) == (0, D_RET)
    pre_ref = z_ref

    @pl.when(pl.program_id(1) == 0)
    def _start_of_sequence():
        sret_ref[...] = jnp.zeros_like(sret_ref)
        c_ref[...] = jnp.zeros_like(c_ref)
        m_ref[...] = jnp.zeros_like(m_ref)
        qkraw_ref[0:SUBLANES, :] = jnp.zeros((SUBLANES, qkraw_ref.shape[1]), F32)
        for hh in range(MLSTM_HEADS):
            ones_cols = slice((2 * hh + 1) * HEAD_DIM, (2 * hh + 2) * HEAD_DIM)
            mv_ref[:, ones_cols] = jnp.ones((rows_total, HEAD_DIM), BF16)

    mod = mod_ref[...]
    h_ref[...] = _modulated_norm(x_ref[...], mod, g_ref[...]).astype(BF16)

    def project(c0, width=D_RET):
        return _dot(h_ref[...], win_ref[:, c0:c0 + width])

    row_id = lax.broadcasted_iota(jnp.int32, (CHUNK, CHUNK), 0)
    col_id = lax.broadcasted_iota(jnp.int32, (CHUNK, CHUNK), 1)
    causal = row_id >= col_id

    def project_to_z(col, zcol, width=D_RET):
        z_ref[:, zcol:zcol + width] = project(col, width)

    def project_qk(c0):
        qkraw_ref[SUBLANES:SUBLANES + rows_total, c0:c0 + D_RET] = project(COL_MQK + c0)

    def project_gates():
        zg = _dot(h_ref[...], wgate_ref[...]) + gbias_ref[...]
        lane = lax.broadcasted_iota(jnp.int32, zg.shape, 1)
        gate_ref[...] = jnp.where(lane < MLSTM_HEADS, zg, _log_sigmoid(zg))

    def project_rv():
        rv_ref[...] = project(COL_RV).astype(BF16)

    def project_mv():
        mv = project(COL_MV)
        for hh in range(MLSTM_HEADS):
            mv_ref[:, 2 * hh * HEAD_DIM:(2 * hh + 1) * HEAD_DIM] = (
                mv[:, hh * HEAD_DIM:(hh + 1) * HEAD_DIM].astype(BF16))

    def stage_retention(r):
        cos = cos_ref[r * CHUNK:(r + 1) * CHUNK, :]
        sin = sin_ref[r * CHUNK:(r + 1) * CHUNK, :]
        for hh in range(RET_HEADS):
            _stage_retention(r, hh, z_ref, cos, sin, wq_ref, wk_ref, rq_ref, rqw_ref, rk_ref,
                             rkwt_ref)

    def stage_mlstm_q(r):
        for blk in range(MLSTM_HEADS):
            _stage_mlstm_qk(r, blk, qkraw_ref, convw_ref, convb_ref, mq_ref, mk_ref, mkt_ref)

    def stage_mlstm_k(r):
        for blk in range(MLSTM_HEADS, 2 * MLSTM_HEADS):
            _stage_mlstm_qk(r, blk, qkraw_ref, convw_ref, convb_ref, mq_ref, mk_ref, mkt_ref)

    def stage_gates(r):
        _stage_gates(r, gate_ref, row_id, cmb_ref, bb_ref, arow_ref)

    chunks = range(n_chunks)
    half = (n_chunks + 1) // 2
    emission = [
        ([functools.partial(project_qk, 0)], []),
        ([functools.partial(project_qk, D_RET)], [(stage_mlstm_q, r) for r in chunks[:half]]),
        ([project_gates, functools.partial(project_to_z, COL_RQ, Z_RQ)],
         [(stage_mlstm_q, r) for r in chunks[half:]]),
        ([functools.partial(project_to_z, COL_RK, Z_RK)],
         [(stage_mlstm_k, r) for r in chunks[:half]]),
        ([project_rv], [(stage_mlstm_k, r) for r in chunks[half:]] + [(stage_gates, r) for r in chunks]),
        ([project_mv], [(stage_retention, 0)]),
    ]
    for run_projections, run_stages in emission:
        for run_projection in run_projections:
            run_projection()
        for stage, r in run_stages:
            stage(r)
    qkraw_ref[0:SUBLANES, :] = qkraw_ref[rows_total:rows_total + SUBLANES, :]

    tasks = []
    for r in range(n_chunks):
        for hh in range(max(RET_HEADS, MLSTM_HEADS)):
            if hh < MLSTM_HEADS:
                tasks.append(_mlstm_chunk(hh, r, causal, mq_ref, mk_ref, mv_ref, mkt_ref,
                                          cmb_ref, bb_ref, arow_ref, c_ref, m_ref, pre_ref))
            if hh < RET_HEADS:
                tasks.append(_retention_chunk(hh, r, rq_ref, rqw_ref, rk_ref, rv_ref, rkwt_ref,
                                              decay_ref, cdec_ref, sret_ref, pre_ref))
    n_stages = 3
    n_steps = len(tasks) + n_stages - 1
    tasks_per_chunk = len(tasks) // n_chunks
    fillers = {}
    for r in range(1, n_chunks):
        fillers.setdefault((r - 1) * tasks_per_chunk, []).append(
            functools.partial(stage_retention, r))
    late = [(col + c0, zcol + c0) for col, zcol in ((COL_RG, Z_RG), (COL_MO, Z_MO))
            for c0 in range(0, D_RET, MXU_COLS)]
    for k, (col, zcol) in enumerate(late):
        fillers.setdefault(1 + k * (n_steps // len(late)), []).append(
            functools.partial(project_to_z, col, zcol, MXU_COLS))
    _run_staggered(tasks, n_stages, fillers)

    for hh in range(RET_HEADS + MLSTM_HEADS):
        is_retention = hh < RET_HEADS
        c0 = (hh if is_retention else hh - RET_HEADS) * HEAD_DIM
        pre = pre_ref[:, hh * HEAD_DIM:(hh + 1) * HEAD_DIM]
        if is_retention:
            hn = (jax.nn.silu(z_ref[:, Z_RG + c0:Z_RG + c0 + HEAD_DIM])
                  * _head_norm(pre, gret_ref[:, c0:c0 + HEAD_DIM]))
        else:
            gate = jax.nn.sigmoid(z_ref[:, Z_MO + c0:Z_MO + c0 + HEAD_DIM])
            hn = _head_norm(gate * pre, gml_ref[:, c0:c0 + HEAD_DIM])
        y_ref[:, hh * HEAD_DIM:(hh + 1) * HEAD_DIM] = hn.astype(y_ref.dtype)
    o_ref[...] = x_ref[...] + mod[2:3] * _dot(y_ref[...], wout_ref[...])


def _retention_constants():
    heads = jnp.arange(RET_HEADS, dtype=F32)
    log_gamma = jnp.log(1.0 - 2.0 ** (-5.0 - heads))
    idx = jnp.arange(CHUNK)
    diff = (idx[:, None] - idx[None, :]).astype(F32)
    decay = jnp.where(diff >= 0, jnp.exp(log_gamma[:, None, None] * jnp.maximum(diff, 0.0)), 0.0)
    w_k = jnp.exp(log_gamma[:, None] * (CHUNK - 1 - idx).astype(F32))
    w_q = jnp.exp(log_gamma[:, None] * (idx + 1).astype(F32))
    chunk_decay = jnp.exp(log_gamma * CHUNK)
    bcast = lambda t: jnp.broadcast_to(t[:, :, None], (RET_HEADS, CHUNK, HEAD_DIM))
    cdec = jnp.broadcast_to(chunk_decay[:, None, None], (RET_HEADS, 1, HEAD_DIM))
    return decay, bcast(w_q), bcast(w_k), cdec


def _gate_weights(w_in):
    n_gates = w_in.shape[-1] - COL_GATES
    return jnp.pad(w_in[:, :, COL_GATES:], ((0, 0), (0, 0), (0, LANES - n_gates))).astype(BF16)


def _mixer(x, mod, g, layer, w_main, w_gate, conv_w, conv_b, b_igate, b_fgate, g_ret, g_ml, w_out,
           cos, sin, rows):
    b, s, d = x.shape
    d_mix = D_RET + D_MLSTM
    n_gates = 2 * MLSTM_HEADS
    n_chunks = rows // CHUNK
    assert w_main.shape[1] == d and w_main.shape[2] >= COL_GATES and w_out.shape[1:] == (d_mix, d)
    gbias = jnp.pad(jnp.concatenate([b_igate, b_fgate]), (0, LANES - n_gates)).reshape(1, LANES)
    decay, w_q, w_k, cdec = _retention_constants()

    def resident(shape):
        zeros = (0,) * len(shape)
        return pl.BlockSpec(shape, lambda i, j: zeros, pipeline_mode=pl.Buffered(1))

    def layer_weight(rows_, cols_):
        return pl.BlockSpec((None, rows_, cols_), lambda i, j: (layer, 0, 0),
                            pipeline_mode=pl.Buffered(1))

    tile = lambda width: pl.BlockSpec((None, rows, width), lambda i, j: (i, j, 0))
    in_specs = [
        tile(d),
        pl.BlockSpec((None, 3, d), lambda i, j: (i, 0, 0)),
        resident((1, d)),
        layer_weight(d, COL_GATES),
        layer_weight(d, LANES),
        resident((1, LANES)),
        tile(HEAD_DIM),
        tile(HEAD_DIM),
        resident((CONV_WIDTH, 2 * D_MLSTM)),
        resident((1, 2 * D_MLSTM)),
        resident((RET_HEADS, CHUNK, CHUNK)),
        resident((RET_HEADS, CHUNK, HEAD_DIM)),
        resident((RET_HEADS, CHUNK, HEAD_DIM)),
        resident((RET_HEADS, 1, HEAD_DIM)),
        resident((1, D_RET)),
        resident((1, D_MLSTM)),
        layer_weight(d_mix, d),
    ]
    head_tiles = lambda heads, dtype: pltpu.VMEM((n_chunks, heads, HEAD_DIM, CHUNK), dtype)
    scratch_shapes = [
        pltpu.VMEM((rows, d), BF16),
        pltpu.VMEM((rows, Z_WIDTH), F32),
        pltpu.VMEM((rows + SUBLANES, 2 * D_MLSTM), F32),
        pltpu.VMEM((rows, LANES), F32),
        pltpu.VMEM((rows, d_mix), BF16),
        pltpu.VMEM((rows, D_RET), BF16),
        pltpu.VMEM((rows, D_RET), BF16),
        pltpu.VMEM((rows, D_RET), BF16),
        pltpu.VMEM((rows, D_RET), BF16),
        head_tiles(RET_HEADS, BF16),
        pltpu.VMEM((rows, D_MLSTM), BF16),
        pltpu.VMEM((rows, D_MLSTM), BF16),
        pltpu.VMEM((rows, 2 * D_MLSTM), BF16),
        head_tiles(MLSTM_HEADS, F32),
        pltpu.VMEM((rows, D_MLSTM), F32),
        pltpu.VMEM((rows, D_MLSTM), F32),
        pltpu.VMEM((n_chunks, SUBLANES, CHUNK), F32),
        pltpu.VMEM((RET_HEADS, HEAD_DIM, HEAD_DIM), F32),
        pltpu.VMEM((MLSTM_HEADS, HEAD_DIM, 2 * HEAD_DIM), F32),
        pltpu.VMEM((MLSTM_HEADS, 1, LANES), F32),
    ]
    return pl.pallas_call(
        _mixer_kernel,
        grid=(b, s // rows),
        in_specs=in_specs,
        out_specs=tile(d),
        out_shape=jax.ShapeDtypeStruct(x.shape, x.dtype),
        scratch_shapes=scratch_shapes,
        compiler_params=pltpu.CompilerParams(
            dimension_semantics=("arbitrary", "arbitrary"), vmem_limit_bytes=VMEM_LIMIT_BYTES),
        name="mixer",
    )(x, mod, g.reshape(1, d), w_main, w_gate, gbias, cos, sin, conv_w, conv_b.reshape(1, -1),
      decay, w_q, w_k, cdec, g_ret.reshape(1, -1), g_ml.reshape(1, -1), w_out)


@jax.jit
def kernel(x, c, positions, norm_g, w_ada, b_ada, w_ff1, w_ff3, w_ff2, w_in, conv_w, conv_b,
           b_igate, b_fgate, g_ret_norm, g_mlstm_norm, w_out, g_final):
    depth = w_in.shape[0]
    ffn_rows, mix_rows, prepare_rows = _tiles(x.shape[1])
    mods = _ada_mod(c, w_ada, b_ada)
    d, d_ff = w_ff1.shape[-2:]
    cos, sin, (*ffn_weights, w_out) = _prepare(
        positions, prepare_rows, [(w_ff1, d_ff), (w_ff3, d_ff), (w_ff2, d), (w_out, d)])
    w_main = w_in.astype(BF16)
    w_gate = _gate_weights(w_main)
    for l in range(depth):
        x = _ffn(x, mods[l, 0], norm_g[l, 0], (l, 0), *ffn_weights, ffn_rows)
        x = _mixer(x, mods[l, 1], norm_g[l, 1], l, w_main, w_gate, conv_w[l], conv_b[l],
                   b_igate[l], b_fgate[l], g_ret_norm[l], g_mlstm_norm[l], w_out, cos, sin,
                   mix_rows)
        x = _ffn(x, mods[l, 2], norm_g[l, 2], (l, 1), *ffn_weights, ffn_rows,
                 g_final=g_final if l == depth - 1 else None)
    return x
```

```python
import functools

import jax
import jax.numpy as jnp
from jax import lax
from jax.experimental import pallas as pl
from jax.experimental.pallas import tpu as pltpu

F32 = jnp.float32
BF16 = jnp.bfloat16

RET_HEADS = 4
MLSTM_HEADS = 4
HEAD_DIM = 128
CHUNK = 128
CONV_WIDTH = 4
ROPE_BASE = 10000.0
EPS = 1e-6
D_RET = RET_HEADS * HEAD_DIM
D_MLSTM = MLSTM_HEADS * HEAD_DIM

LANES = 128
SUBLANES = 8
MXU_COLS = 256
VMEM_LIMIT_BYTES = 56 * 1024 * 1024
MIXER_VMEM_LIMIT_BYTES = 60 * 1024 * 1024

COL_RQ = 0
COL_RK = D_RET
COL_RV = 2 * D_RET
COL_RG = 3 * D_RET
COL_MQK = 4 * D_RET
COL_MV = COL_MQK + 2 * D_MLSTM
COL_MO = COL_MV + D_MLSTM
COL_GATES = COL_MO + D_MLSTM

NT_DIMS = (((1,), (1,)), ((), ()))


def _tiles(seq):
    ffn_rows = min(1024, seq)
    mix_rows = min(1024, seq)
    prepare_rows = seq // 2
    assert seq % ffn_rows == 0 and seq % mix_rows == 0 and mix_rows % CHUNK == 0
    return ffn_rows, mix_rows, prepare_rows


def _dot(a, b):
    return jnp.dot(a, b, preferred_element_type=F32)


def _rmsnorm(x, g):
    return x * lax.rsqrt(jnp.mean(x * x, axis=-1, keepdims=True) + EPS) * g


def _modulated_norm(x, mod, g):
    shift, scale = mod[0:1], mod[1:2]
    return _rmsnorm(x, g) * (1.0 + scale) + shift


def _log_sigmoid(x):
    return jnp.minimum(x, 0.0) - jnp.log1p(jnp.exp(-jnp.abs(x)))


def _fold_to_tile(t):
    acc = None
    for r0 in range(0, t.shape[0], SUBLANES):
        for c0 in range(0, t.shape[1], LANES):
            blk = t[r0:r0 + SUBLANES, c0:c0 + LANES]
            acc = blk if acc is None else acc + blk
    return acc


def _ada_kernel(c_ref, w_ref, b_ref, o_ref):
    sc = jax.nn.silu(c_ref[...])
    w = w_ref[...]
    s_hi = sc.astype(BF16)
    s_lo = (sc - s_hi.astype(F32)).astype(BF16)
    w_hi = w.astype(BF16)
    w_lo = (w - w_hi.astype(F32)).astype(BF16)
    n = sc.shape[0]
    both = _dot(jnp.concatenate([s_hi, s_lo], axis=0), w_hi)
    o_ref[...] = both[:n] + both[n:] + _dot(s_hi, w_lo) + b_ref[...]


def _ada_mod(c, w_ada, b_ada):
    depth, nsub, d, d3 = w_ada.shape
    n = depth * nsub
    b = c.shape[0]
    out = pl.pallas_call(
        _ada_kernel,
        grid=(n, d3 // d),
        in_specs=[
            pl.BlockSpec((b, d), lambda i, j: (0, 0)),
            pl.BlockSpec((None, d, d), lambda i, j: (i, 0, j)),
            pl.BlockSpec((None, 1, d), lambda i, j: (i, 0, j)),
        ],
        out_specs=pl.BlockSpec((None, b, d), lambda i, j: (i, 0, j)),
        out_shape=jax.ShapeDtypeStruct((n, b, d3), F32),
        compiler_params=pltpu.CompilerParams(
            dimension_semantics=("arbitrary", "arbitrary"), vmem_limit_bytes=VMEM_LIMIT_BYTES),
        name="ada_mod",
    )(c, w_ada.reshape(n, d, d3), b_ada.reshape(n, 1, d3))
    return out.reshape(depth, nsub, b, d3 // d, d)


def _prepare_kernel(pos_ref, invf_ref, *refs):
    n_weights = (len(refs) - 2) // 2
    weight_refs = refs[:n_weights]
    cos_ref, sin_ref = refs[n_weights:n_weights + 2]
    for src_ref, dst_ref in zip(weight_refs, refs[n_weights + 2:]):
        dst_ref[...] = src_ref[...].astype(BF16)
    rows, half = pos_ref.shape[0], HEAD_DIM // 2
    pos = pos_ref[...].astype(F32)
    low = lax.broadcasted_iota(jnp.int32, (rows // 2, HEAD_DIM), 1) < half
    ang = jnp.where(low, pos[:rows // 2], pos[rows // 2:]) * invf_ref[...]
    c, s = jnp.cos(ang), jnp.sin(ang)
    c_swapped, s_swapped = pltpu.roll(c, half, 1), pltpu.roll(s, half, 1)
    cos_ref[:rows // 2, :] = jnp.where(low, c, c_swapped)
    cos_ref[rows // 2:, :] = jnp.where(low, c_swapped, c)
    sin_ref[:rows // 2, :] = jnp.where(low, -s, s_swapped)
    sin_ref[rows // 2:, :] = jnp.where(low, -s_swapped, s)


def _prepare(positions, rows, weights):
    b, s = positions.shape
    assert rows % (2 * SUBLANES) == 0 and s % rows == 0
    n_j = s // rows
    n_steps = b * n_j
    inv_freq = ROPE_BASE ** (-jnp.arange(0, HEAD_DIM, 2, dtype=F32) / HEAD_DIM)
    invf = jnp.concatenate([inv_freq, inv_freq]).reshape(1, HEAD_DIM)
    table = jax.ShapeDtypeStruct((b, s, HEAD_DIM), F32)
    const = pl.BlockSpec((1, HEAD_DIM), lambda i, j: (0, 0))
    tile = pl.BlockSpec((None, rows, HEAD_DIM), lambda i, j: (i, j, 0))
    flat, weight_specs, cast_shapes = [], [], []
    for w, n_cols in weights:
        w2d = w.reshape(-1, w.shape[-1])
        block_rows = w2d.shape[0] // n_steps
        packed = 2 * SUBLANES
        assert block_rows * n_steps == w2d.shape[0] and block_rows % packed == 0
        assert n_cols % LANES == 0
        flat.append(w2d)
        weight_specs.append(pl.BlockSpec((block_rows, n_cols), lambda i, j: (i * n_j + j, 0)))
        cast_shapes.append(jax.ShapeDtypeStruct((w2d.shape[0], n_cols), BF16))
    outs = pl.pallas_call(
        _prepare_kernel,
        grid=(b, n_j),
        in_specs=[pl.BlockSpec((None, rows, 1), lambda i, j: (i, j, 0)), const] + weight_specs,
        out_specs=[tile, tile] + weight_specs,
        out_shape=[table, table] + cast_shapes,
        compiler_params=pltpu.CompilerParams(
            dimension_semantics=("arbitrary", "arbitrary"), vmem_limit_bytes=VMEM_LIMIT_BYTES),
        name="prepare",
    )(positions.reshape(b, s, 1), invf, *flat)
    casts = [o.reshape(*w.shape[:-1], n_cols) for o, (w, n_cols) in zip(outs[2:], weights)]
    return outs[0], outs[1], casts


NORM_PIECES = 8


def _ffn_kernel(x_ref, xnext_ref, mod_ref, modnext_ref, g_ref, w1_ref, w3_ref, w2_ref, *rest,
                final_norm):
    if final_norm:
        gf_ref, o_ref, h_even_ref, h_odd_ref, act_ref = rest
    else:
        o_ref, h_even_ref, h_odd_ref, act_ref = rest
    t = pl.program_id(0)
    rows = x_ref.shape[0]
    d_ff = w1_ref.shape[1]

    def normalise(src_ref, src_mod_ref, dst_ref, r0, n):
        hn = _modulated_norm(src_ref[r0:r0 + n, :], src_mod_ref[...], g_ref[...])
        dst_ref[r0:r0 + n, :] = hn.astype(BF16)
        return hn

    @pl.when(t == 0)
    def _first_tile():
        normalise(x_ref, mod_ref, h_even_ref, 0, rows)

    def tile_step(h_ref, hnext_ref):
        piece = rows // NORM_PIECES
        pieces = list(range(0, rows, piece))
        anchor = None
        for c0 in range(0, d_ff, MXU_COLS):
            h = h_ref[...]
            a = _dot(h, w1_ref[:, c0:c0 + MXU_COLS])
            b = _dot(h, w3_ref[:, c0:c0 + MXU_COLS])
            act = jax.nn.silu(a) * b
            act_ref[:, c0:c0 + MXU_COLS] = act.astype(BF16)
            if anchor is not None:
                packed = 2 * SUBLANES
                act_ref[0:packed, c0:c0 + LANES] = (
                    act[0:packed, 0:LANES]
                    + 0.0 * jnp.concatenate([anchor, anchor], axis=0)).astype(BF16)
                anchor = None
            if pieces:
                anchor = _fold_to_tile(
                    normalise(xnext_ref, modnext_ref, hnext_ref, pieces.pop(0), piece))
        assert not pieces and anchor is None
        out = x_ref[...] + 0.5 * mod_ref[2:3, :] * _dot(act_ref[...], w2_ref[...])
        if final_norm:
            out = _rmsnorm(out, gf_ref[...])
        o_ref[...] = out

    @pl.when(t % 2 == 0)
    def _even_tile():
        tile_step(h_even_ref, h_odd_ref)

    @pl.when(t % 2 == 1)
    def _odd_tile():
        tile_step(h_odd_ref, h_even_ref)


def _ffn(x, mod, g, which, w1, w3, w2, rows, g_final=None):
    b, s, d = x.shape
    d_ff = w1.shape[-1]
    assert d_ff % MXU_COLS == 0 and rows % (NORM_PIECES * SUBLANES) == 0
    final_norm = g_final is not None
    tiles_per_seq = s // rows
    n_tiles = b * tiles_per_seq
    following = lambda t: jnp.minimum(t + 1, n_tiles - 1)
    resident = lambda shape: pl.BlockSpec(shape, lambda t: (0, 0), pipeline_mode=pl.Buffered(1))
    weight = lambda rows_, cols_: pl.BlockSpec((None, None, rows_, cols_),
                                               lambda t: (*which, 0, 0),
                                               pipeline_mode=pl.Buffered(1))
    tile = pl.BlockSpec((None, rows, d), lambda t: (t, 0, 0))
    in_specs = [
        tile,
        pl.BlockSpec((None, rows, d), lambda t: (following(t), 0, 0)),
        pl.BlockSpec((None, 3, d), lambda t: (t // tiles_per_seq, 0, 0)),
        pl.BlockSpec((None, 3, d), lambda t: (following(t) // tiles_per_seq, 0, 0)),
        resident((1, d)),
        weight(d, d_ff),
        weight(d, d_ff),
        weight(d_ff, d),
    ]
    x_tiles = x.reshape(n_tiles, rows, d)
    args = [x_tiles, x_tiles, mod, mod, g.reshape(1, d), w1, w3, w2]
    if final_norm:
        in_specs.append(resident((1, d)))
        args.append(g_final.reshape(1, d))
    out = pl.pallas_call(
        functools.partial(_ffn_kernel, final_norm=final_norm),
        grid=(n_tiles,),
        in_specs=in_specs,
        out_specs=tile,
        out_shape=jax.ShapeDtypeStruct(x_tiles.shape, x.dtype),
        scratch_shapes=[pltpu.VMEM((rows, d), BF16), pltpu.VMEM((rows, d), BF16),
                        pltpu.VMEM((rows, d_ff), BF16)],
        compiler_params=pltpu.CompilerParams(
            dimension_semantics=("arbitrary",), vmem_limit_bytes=VMEM_LIMIT_BYTES),
        name="ffn_final" if final_norm else "ffn",
    )(*args)
    return out.reshape(x.shape)


Z_RQ = 0
Z_RK = D_RET
Z_RG = 2 * D_RET
Z_MO = 3 * D_RET
Z_WIDTH = 3 * D_RET + D_MLSTM


def _head_norm(h, g):
    mu = jnp.mean(h, axis=-1, keepdims=True)
    d = h - mu
    var = jnp.mean(d * d, axis=-1, keepdims=True)
    return d * lax.rsqrt(var + EPS) * g


def _prefix_scan_rows(t, row_id, combine, identity):
    shift = 1
    while shift < CHUNK:
        t = combine(t, jnp.where(row_id >= shift, pltpu.roll(t, shift, 0), identity))
        shift *= 2
    return t


def _stage_retention(r, hh, z_ref, cos, sin, wq_ref, wk_ref, rq_ref, rqw_ref, rk_ref, rkwt_ref):
    r0, c0 = r * CHUNK, hh * HEAD_DIM
    q = z_ref[r0:r0 + CHUNK, Z_RQ + c0:Z_RQ + c0 + HEAD_DIM]
    k = z_ref[r0:r0 + CHUNK, Z_RK + c0:Z_RK + c0 + HEAD_DIM]
    half = HEAD_DIM // 2
    qr = q * cos + pltpu.roll(q, half, 1) * sin
    kr = (k * cos + pltpu.roll(k, half, 1) * sin) * (HEAD_DIM ** -0.5)
    rq_ref[r0:r0 + CHUNK, c0:c0 + HEAD_DIM] = qr.astype(BF16)
    rqw_ref[r0:r0 + CHUNK, c0:c0 + HEAD_DIM] = (qr * wq_ref[hh]).astype(BF16)
    rk_ref[r0:r0 + CHUNK, c0:c0 + HEAD_DIM] = kr.astype(BF16)
    rkwt_ref[r, hh] = (kr * wk_ref[hh]).T.astype(BF16)


def _stage_mlstm_qk(r, blk, qkraw_ref, convw_ref, convb_ref, mq_ref, mk_ref, mkt_ref):
    r0, c0 = r * CHUNK, blk * HEAD_DIM
    acc = convb_ref[:, c0:c0 + HEAD_DIM]
    for j in range(CONV_WIDTH):
        start = SUBLANES - (CONV_WIDTH - 1) + j + r0
        acc = acc + (convw_ref[j:j + 1, c0:c0 + HEAD_DIM]
                     * qkraw_ref[start:start + CHUNK, c0:c0 + HEAD_DIM])
    qk = jax.nn.silu(acc)
    if blk < MLSTM_HEADS:
        mq_ref[r0:r0 + CHUNK, c0:c0 + HEAD_DIM] = (qk * (HEAD_DIM ** -0.5)).astype(BF16)
    else:
        hh = blk - MLSTM_HEADS
        mk_ref[r0:r0 + CHUNK, hh * HEAD_DIM:(hh + 1) * HEAD_DIM] = qk.astype(BF16)
        mkt_ref[r, hh] = qk.T


def _stage_gates(r, gate_ref, row_id, cmb_ref, bb_ref, arow_ref):
    r0 = r * CHUNK
    gates = gate_ref[r0:r0 + CHUNK, :]
    bcum = _prefix_scan_rows(gates, row_id, jnp.add, 0.0)
    a = gates - pltpu.roll(bcum, LANES - MLSTM_HEADS, 1)
    cmax = _prefix_scan_rows(a, row_id, jnp.maximum, -jnp.inf)
    arow_ref[r] = a.T[0:SUBLANES, :]
    for hh in range(MLSTM_HEADS):
        cols = slice(hh * HEAD_DIM, (hh + 1) * HEAD_DIM)
        cmb_ref[r0:r0 + CHUNK, cols] = jnp.broadcast_to(cmax[:, hh:hh + 1], (CHUNK, HEAD_DIM))
        fcol = MLSTM_HEADS + hh
        bb_ref[r0:r0 + CHUNK, cols] = jnp.broadcast_to(bcum[:, fcol:fcol + 1], (CHUNK, HEAD_DIM))


def _retention_chunk(hh, r, rq_ref, rqw_ref, rk_ref, rv_ref, rkwt_ref, decay_ref, cdec_ref,
                     state_ref, pre_ref):
    rows = slice(r * CHUNK, (r + 1) * CHUNK)
    cols = slice(hh * HEAD_DIM, (hh + 1) * HEAD_DIM)
    scores = lax.dot_general(rq_ref[rows, cols], rk_ref[rows, cols], NT_DIMS,
                             preferred_element_type=F32)
    yield
    vb = rv_ref[rows, cols]
    state = state_ref[hh]
    intra = _dot((scores * decay_ref[hh]).astype(BF16), vb)
    inter = _dot(rqw_ref[rows, cols], state.astype(BF16))
    update = _dot(rkwt_ref[r, hh], vb)
    yield
    pre_ref[rows, cols] = intra + inter
    state_ref[hh] = state * cdec_ref[hh] + update


def _mlstm_chunk(hh, r, causal, mq_ref, mk_ref, mv_ref, mkt_ref, cmb_ref, bb_ref, arow_ref,
                 c_ref, m_ref, pre_ref):
    rows = slice(r * CHUNK, (r + 1) * CHUNK)
    c0 = hh * HEAD_DIM
    cols = slice(c0, c0 + HEAD_DIM)
    qb = mq_ref[rows, cols]
    scores = lax.dot_general(qb, mk_ref[rows, cols], NT_DIMS, preferred_element_type=F32)
    yield
    v_aug = mv_ref[rows, 2 * c0:2 * c0 + 2 * HEAD_DIM]
    cmax = cmb_ref[rows, cols]
    b_l = bb_ref[rows, cols]
    a_row = arow_ref[r][hh:hh + 1, :]
    m_prev = m_ref[hh]
    state = c_ref[hh]
    mx = jnp.maximum(cmax, m_prev)
    w_d = jnp.exp(jnp.where(causal, a_row - mx, -jnp.inf))
    intra = _dot((scores * w_d).astype(BF16), v_aug)
    inter = _dot(qb, state.astype(BF16))
    b_last = b_l[CHUNK - 1:CHUNK, :]
    m_new = jnp.maximum(b_last + m_prev, b_last + cmax[CHUNK - 1:CHUNK, :])
    w_g = jnp.exp(b_last + a_row - m_new)
    update = _dot((mkt_ref[r, hh] * w_g).astype(BF16), v_aug)
    yield
    w_inter = jnp.exp(m_prev - mx)
    tot = intra + jnp.concatenate([w_inter, w_inter], axis=1) * inter
    num, den = tot[:, :HEAD_DIM], tot[:, HEAD_DIM:]
    h = num / jnp.maximum(jnp.abs(den), jnp.exp(-(b_l + mx)))
    pre_ref[rows, D_RET + c0:D_RET + c0 + HEAD_DIM] = h
    w_c = jnp.exp(b_last + m_prev - m_new)
    c_ref[hh] = jnp.concatenate([w_c, w_c], axis=1) * state + update
    m_ref[hh] = m_new


def _run_staggered(tasks, stages, fillers):
    n = len(tasks)
    n_steps = n + stages - 1
    for step in range(n_steps):
        for run_filler in fillers.get(step, ()):
            run_filler()
        for s in range(stages):
            j = step - s
            if 0 <= j < n:
                next(tasks[j], None)
    assert all(step < n_steps for step in fillers)


def _mixer_kernel(x_ref, mod_ref, g_ref, win_ref, wgate_ref, gbias_ref, cos_ref, sin_ref, convw_ref,
                  convb_ref,
                  decay_ref, wq_ref, wk_ref, cdec_ref, gret_ref, gml_ref, wout_ref,
                  o_ref,
                  h_ref, z_ref, qkraw_ref, gate_ref, y_ref,
                  rq_ref, rqw_ref, rk_ref, rv_ref, rkwt_ref, mq_ref, mk_ref, mv_ref, mkt_ref,
                  cmb_ref, bb_ref, arow_ref, sret_ref, c_ref, m_ref):
    rows_total = x_ref.shape[0]
    n_chunks = rows_total // CHUNK
    assert (Z_RQ, Z_RK) == (0, D_RET)
    pre_ref = z_ref

    @pl.when(pl.program_id(1) == 0)
    def _start_of_sequence():
        sret_ref[...] = jnp.zeros_like(sret_ref)
        c_ref[...] = jnp.zeros_like(c_ref)
        m_ref[...] = jnp.zeros_like(m_ref)
        qkraw_ref[0:SUBLANES, :] = jnp.zeros((SUBLANES, qkraw_ref.shape[1]), F32)
        for hh in range(MLSTM_HEADS):
            ones_cols = slice((2 * hh + 1) * HEAD_DIM, (2 * hh + 2) * HEAD_DIM)
            mv_ref[:, ones_cols] = jnp.ones((rows_total, HEAD_DIM), BF16)

    mod = mod_ref[...]
    h_ref[...] = _modulated_norm(x_ref[...], mod, g_ref[...]).astype(BF16)

    def project(c0, width=D_RET):
        return _dot(h_ref[...], win_ref[:, c0:c0 + width])

    row_id = lax.broadcasted_iota(jnp.int32, (CHUNK, CHUNK), 0)
    col_id = lax.broadcasted_iota(jnp.int32, (CHUNK, CHUNK), 1)
    causal = row_id >= col_id

    def project_to_z(col, zcol, width=D_RET):
        z_ref[:, zcol:zcol + width] = project(col, width)

    def project_qk(c0):
        qkraw_ref[SUBLANES:SUBLANES + rows_total, c0:c0 + D_RET] = project(COL_MQK + c0)

    def project_gates():
        zg = _dot(h_ref[...], wgate_ref[...]) + gbias_ref[...]
        lane = lax.broadcasted_iota(jnp.int32, zg.shape, 1)
        gate_ref[...] = jnp.where(lane < MLSTM_HEADS, zg, _log_sigmoid(zg))

    def project_rv():
        rv_ref[...] = project(COL_RV).astype(BF16)

    def project_mv():
        mv = project(COL_MV)
        for hh in range(MLSTM_HEADS):
            mv_ref[:, 2 * hh * HEAD_DIM:(2 * hh + 1) * HEAD_DIM] = (
                mv[:, hh * HEAD_DIM:(hh + 1) * HEAD_DIM].astype(BF16))

    def stage_retention(r):
        cos = cos_ref[r * CHUNK:(r + 1) * CHUNK, :]
        sin = sin_ref[r * CHUNK:(r + 1) * CHUNK, :]
        for hh in range(RET_HEADS):
            _stage_retention(r, hh, z_ref, cos, sin, wq_ref, wk_ref, rq_ref, rqw_ref, rk_ref,
                             rkwt_ref)

    def stage_mlstm_q(r):
        for blk in range(MLSTM_HEADS):
            _stage_mlstm_qk(r, blk, qkraw_ref, convw_ref, convb_ref, mq_ref, mk_ref, mkt_ref)

    def stage_mlstm_k(r):
        for blk in range(MLSTM_HEADS, 2 * MLSTM_HEADS):
            _stage_mlstm_qk(r, blk, qkraw_ref, convw_ref, convb_ref, mq_ref, mk_ref, mkt_ref)

    def stage_gates(r):
        _stage_gates(r, gate_ref, row_id, cmb_ref, bb_ref, arow_ref)

    chunks = range(n_chunks)
    half = (n_chunks + 1) // 2
    emission = [
        ([functools.partial(project_qk, 0)], []),
        ([functools.partial(project_qk, D_RET)], [(stage_mlstm_q, r) for r in chunks[:half]]),
        ([project_gates, functools.partial(project_to_z, COL_RQ, Z_RQ)],
         [(stage_mlstm_q, r) for r in chunks[half:]]),
        ([functools.partial(project_to_z, COL_RK, Z_RK)],
         [(stage_mlstm_k, r) for r in chunks[:half]]),
        ([project_rv], [(stage_mlstm_k, r) for r in chunks[half:]] + [(stage_gates, r) for r in chunks]),
        ([project_mv], [(stage_retention, 0)]),
    ]
    for run_projections, run_stages in emission:
        for run_projection in run_projections:
            run_projection()
        for stage, r in run_stages:
            stage(r)
    qkraw_ref[0:SUBLANES, :] = qkraw_ref[rows_total:rows_total + SUBLANES, :]

    tasks = []
    for r in range(n_chunks):
        for hh in range(max(RET_HEADS, MLSTM_HEADS)):
            if hh < MLSTM_HEADS:
                tasks.append(_mlstm_chunk(hh, r, causal, mq_ref, mk_ref, mv_ref, mkt_ref,
                                          cmb_ref, bb_ref, arow_ref, c_ref, m_ref, pre_ref))
            if hh < RET_HEADS:
                tasks.append(_retention_chunk(hh, r, rq_ref, rqw_ref, rk_ref, rv_ref, rkwt_ref,
                                              decay_ref, cdec_ref, sret_ref, pre_ref))
    n_stages = 3
    n_steps = len(tasks) + n_stages - 1
    tasks_per_chunk = len(tasks) // n_chunks
    fillers = {}
    for r in range(1, n_chunks):
        fillers.setdefault((r - 1) * tasks_per_chunk, []).append(
            functools.partial(stage_retention, r))
    late = [(col + c0, zcol + c0) for col, zcol in ((COL_RG, Z_RG), (COL_MO, Z_MO))
            for c0 in range(0, D_RET, MXU_COLS)]
    for k, (col, zcol) in enumerate(late):
        fillers.setdefault(1 + k * (n_steps // len(late)), []).append(
            functools.partial(project_to_z, col, zcol, MXU_COLS))
    _run_staggered(tasks, n_stages, fillers)

    for hh in range(RET_HEADS + MLSTM_HEADS):
        is_retention = hh < RET_HEADS
        c0 = (hh if is_retention else hh - RET_HEADS) * HEAD_DIM
        pre = pre_ref[:, hh * HEAD_DIM:(hh + 1) * HEAD_DIM]
        if is_retention:
            hn = (jax.nn.silu(z_ref[:, Z_RG + c0:Z_RG + c0 + HEAD_DIM])
                  * _head_norm(pre, gret_ref[:, c0:c0 + HEAD_DIM]))
        else:
            gate = jax.nn.sigmoid(z_ref[:, Z_MO + c0:Z_MO + c0 + HEAD_DIM])
            hn = _head_norm(gate * pre, gml_ref[:, c0:c0 + HEAD_DIM])
        y_ref[:, hh * HEAD_DIM:(hh + 1) * HEAD_DIM] = hn.astype(y_ref.dtype)
    o_ref[...] = x_ref[...] + mod[2:3] * _dot(y_ref[...], wout_ref[...])


def _retention_constants():
    heads = jnp.arange(RET_HEADS, dtype=F32)
    log_gamma = jnp.log(1.0 - 2.0 ** (-5.0 - heads))
    idx = jnp.arange(CHUNK)
    diff = (idx[:, None] - idx[None, :]).astype(F32)
    decay = jnp.where(diff >= 0, jnp.exp(log_gamma[:, None, None] * jnp.maximum(diff, 0.0)), 0.0)
    w_k = jnp.exp(log_gamma[:, None] * (CHUNK - 1 - idx).astype(F32))
    w_q = jnp.exp(log_gamma[:, None] * (idx + 1).astype(F32))
    chunk_decay = jnp.exp(log_gamma * CHUNK)
    bcast = lambda t: jnp.broadcast_to(t[:, :, None], (RET_HEADS, CHUNK, HEAD_DIM))
    cdec = jnp.broadcast_to(chunk_decay[:, None, None], (RET_HEADS, 1, HEAD_DIM))
    return decay, bcast(w_q), bcast(w_k), cdec


def _gate_weights(w_in):
    n_gates = w_in.shape[-1] - COL_GATES
    return jnp.pad(w_in[:, :, COL_GATES:], ((0, 0), (0, 0), (0, LANES - n_gates))).astype(BF16)


def _mixer(x, mod, g, layer, w_main, w_gate, conv_w, conv_b, b_igate, b_fgate, g_ret, g_ml, w_out,
           cos, sin, rows):
    b, s, d = x.shape
    d_mix = D_RET + D_MLSTM
    n_gates = 2 * MLSTM_HEADS
    n_chunks = rows // CHUNK
    assert w_main.shape[1] == d and w_main.shape[2] >= COL_GATES and w_out.shape[1:] == (d_mix, d)
    gbias = jnp.pad(jnp.concatenate([b_igate, b_fgate]), (0, LANES - n_gates)).reshape(1, LANES)
    decay, w_q, w_k, cdec = _retention_constants()

    def resident(shape):
        zeros = (0,) * len(shape)
        return pl.BlockSpec(shape, lambda i, j: zeros, pipeline_mode=pl.Buffered(1))

    def layer_weight(rows_, cols_):
        return pl.BlockSpec((None, rows_, cols_), lambda i, j: (layer, 0, 0),
                            pipeline_mode=pl.Buffered(1))

    tile = lambda width: pl.BlockSpec((None, rows, width), lambda i, j: (i, j, 0))
    in_specs = [
        tile(d),
        pl.BlockSpec((None, 3, d), lambda i, j: (i, 0, 0)),
        resident((1, d)),
        layer_weight(d, COL_GATES),
        layer_weight(d, LANES),
        resident((1, LANES)),
        tile(HEAD_DIM),
        tile(HEAD_DIM),
        resident((CONV_WIDTH, 2 * D_MLSTM)),
        resident((1, 2 * D_MLSTM)),
        resident((RET_HEADS, CHUNK, CHUNK)),
        resident((RET_HEADS, CHUNK, HEAD_DIM)),
        resident((RET_HEADS, CHUNK, HEAD_DIM)),
        resident((RET_HEADS, 1, HEAD_DIM)),
        resident((1, D_RET)),
        resident((1, D_MLSTM)),
        layer_weight(d_mix, d),
    ]
    head_tiles = lambda heads, dtype: pltpu.VMEM((n_chunks, heads, HEAD_DIM, CHUNK), dtype)
    scratch_shapes = [
        pltpu.VMEM((rows, d), BF16),
        pltpu.VMEM((rows, Z_WIDTH), F32),
        pltpu.VMEM((rows + SUBLANES, 2 * D_MLSTM), F32),
        pltpu.VMEM((rows, LANES), F32),
        pltpu.VMEM((rows, d_mix), BF16),
        pltpu.VMEM((rows, D_RET), BF16),
        pltpu.VMEM((rows, D_RET), BF16),
        pltpu.VMEM((rows, D_RET), BF16),
        pltpu.VMEM((rows, D_RET), BF16),
        head_tiles(RET_HEADS, BF16),
        pltpu.VMEM((rows, D_MLSTM), BF16),
        pltpu.VMEM((rows, D_MLSTM), BF16),
        pltpu.VMEM((rows, 2 * D_MLSTM), BF16),
        head_tiles(MLSTM_HEADS, F32),
        pltpu.VMEM((rows, D_MLSTM), F32),
        pltpu.VMEM((rows, D_MLSTM), F32),
        pltpu.VMEM((n_chunks, SUBLANES, CHUNK), F32),
        pltpu.VMEM((RET_HEADS, HEAD_DIM, HEAD_DIM), F32),
        pltpu.VMEM((MLSTM_HEADS, HEAD_DIM, 2 * HEAD_DIM), F32),
        pltpu.VMEM((MLSTM_HEADS, 1, LANES), F32),
    ]
    return pl.pallas_call(
        _mixer_kernel,
        grid=(b, s // rows),
        in_specs=in_specs,
        out_specs=tile(d),
        out_shape=jax.ShapeDtypeStruct(x.shape, x.dtype),
        scratch_shapes=scratch_shapes,
        compiler_params=pltpu.CompilerParams(
            dimension_semantics=("arbitrary", "arbitrary"),
            vmem_limit_bytes=MIXER_VMEM_LIMIT_BYTES),
        name="mixer",
    )(x, mod, g.reshape(1, d), w_main, w_gate, gbias, cos, sin, conv_w, conv_b.reshape(1, -1),
      decay, w_q, w_k, cdec, g_ret.reshape(1, -1), g_ml.reshape(1, -1), w_out)


@jax.jit
def kernel(x, c, positions, norm_g, w_ada, b_ada, w_ff1, w_ff3, w_ff2, w_in, conv_w, conv_b,
           b_igate, b_fgate, g_ret_norm, g_mlstm_norm, w_out, g_final):
    depth = w_in.shape[0]
    ffn_rows, mix_rows, prepare_rows = _tiles(x.shape[1])
    mods = _ada_mod(c, w_ada, b_ada)
    d, d_ff = w_ff1.shape[-2:]
    cos, sin, (*ffn_weights, w_out) = _prepare(
        positions, prepare_rows, [(w_ff1, d_ff), (w_ff3, d_ff), (w_ff2, d), (w_out, d)])
    w_main = w_in.astype(BF16)
    w_gate = _gate_weights(w_main)
    for l in range(depth):
        x = _ffn(x, mods[l, 0], norm_g[l, 0], (l, 0), *ffn_weights, ffn_rows)
        x = _mixer(x, mods[l, 1], norm_g[l, 1], l, w_main, w_gate, conv_w[l], conv_b[l],
                   b_igate[l], b_fgate[l], g_ret_norm[l], g_mlstm_norm[l], w_out, cos, sin,
                   mix_rows)
        x = _ffn(x, mods[l, 2], norm_g[l, 2], (l, 1), *ffn_weights, ffn_rows,
                 g_final=g_final if l == depth - 1 else None)
    return x
```
